```python
import jax, jax.numpy as jnp
from jax import lax
import numpy as np

D_MODEL = 1024
BATCH = 8
SEQ = 2048
DEPTH = 4
DEC_BATCH = 128
DEC_SEQ = 4
PAST_LEN = 2048
PAGE_SIZE = 128

N_META = 16
META_BLOCK = 128
META_PAD = META_BLOCK - N_META
N_MIXERS = 2
N_FOX = (DEPTH + 1) // 2
N_GLA = DEPTH // 2
FOX_HEADS = 16
FOX_HEAD_DIM = D_MODEL // FOX_HEADS
FOX_SCALE = FOX_HEAD_DIM ** -0.5
Q_BLOCK = 128
GLA_HEADS = 4
GLA_DK = D_MODEL // 2
GLA_DV = D_MODEL
GLA_DK_H = GLA_DK // GLA_HEADS
GLA_DV_H = GLA_DV // GLA_HEADS
GLA_SCALE = GLA_DK_H ** -0.5
GLA_GATE_RANK = 16
GLA_TAU = 16.0
GLA_CHUNK = 64
D_FF = -(-8 * D_MODEL // (3 * 256)) * 256
RMS_EPS = 1e-6
NEG_INF = -1e30

kernel_name = "fox_gla_hybrid_decode_step"


def rmsnorm(x, g):
    xf = x.astype(jnp.float32)
    y = xf * lax.rsqrt(jnp.mean(xf * xf, axis=-1, keepdims=True) + RMS_EPS)
    return (y * g.astype(jnp.float32)).astype(x.dtype)


def swiglu(xn, w_up, w_down):
    h = xn @ w_up
    return (jax.nn.silu(h[..., :D_FF]) * h[..., D_FF:]) @ w_down


def fox_project(xn, w_in, b_f):
    proj = xn @ w_in
    lead = xn.shape[:-1]
    q = proj[..., :D_MODEL].reshape(*lead, FOX_HEADS, FOX_HEAD_DIM)
    k = proj[..., D_MODEL:2 * D_MODEL].reshape(*lead, FOX_HEADS, FOX_HEAD_DIM)
    v = proj[..., 2 * D_MODEL:3 * D_MODEL].reshape(*lead, FOX_HEADS, FOX_HEAD_DIM)
    logf = jax.nn.log_sigmoid((proj[..., 3 * D_MODEL:] + b_f).astype(jnp.float32))
    return q, k, v, logf


def fox_prompt(xn, valid, w_in, b_f, w_out):
    B, L, _ = xn.shape
    q, k, v, logf = fox_project(xn, w_in, b_f)
    ct = jnp.cumsum(logf, axis=1).transpose(0, 2, 1)
    nb = L // Q_BLOCK
    q_blocks = jnp.moveaxis(q.reshape(B, nb, Q_BLOCK, FOX_HEADS, FOX_HEAD_DIM), 1, 0)
    c_blocks = jnp.moveaxis(ct.reshape(B, FOX_HEADS, nb, Q_BLOCK), 2, 0)
    qpos = jnp.arange(L, dtype=jnp.int32).reshape(nb, Q_BLOCK)
    kpos = jnp.arange(L, dtype=jnp.int32)

    def block(args):
        q_i, c_i, p_i = args
        s = jnp.einsum("bqhd,bkhd->bhqk", q_i, k).astype(jnp.float32) * FOX_SCALE
        s = s + (c_i[..., :, None] - ct[..., None, :])
        mask = (kpos[None, :] <= p_i[:, None]) & valid[None, :]
        s = jnp.where(mask, s, NEG_INF)
        p = jax.nn.softmax(s, axis=-1).astype(v.dtype)
        return jnp.einsum("bhqk,bkhd->bqhd", p, v)

    o = lax.map(block, (q_blocks, c_blocks, qpos))
    o = jnp.moveaxis(o, 0, 1).reshape(B, L, D_MODEL)
    return o @ w_out, k, v, logf


def fox_sample(xn, k_past, v_past, logf_past, w_in, b_f, w_out):
    DB, T, _ = xn.shape
    q, k, v, logf = fox_project(xn, w_in, b_f)
    c_past = jnp.cumsum(logf_past.astype(jnp.float32), axis=1)
    c_new = c_past[:, -1:, :] + jnp.cumsum(logf, axis=1)
    cpt = c_past.transpose(0, 2, 1)
    cnt = c_new.transpose(0, 2, 1)
    s_past = jnp.einsum("bqhd,bkhd->bhqk", q, k_past).astype(jnp.float32) * FOX_SCALE
    s_past = s_past + (cnt[..., :, None] - cpt[..., None, :])
    s_new = jnp.einsum("bqhd,bkhd->bhqk", q, k).astype(jnp.float32) * FOX_SCALE
    s_new = s_new + (cnt[..., :, None] - cnt[..., None, :])
    tri = jnp.tril(jnp.ones((T, T), dtype=bool))
    s_new = jnp.where(tri, s_new, NEG_INF)
    p = jax.nn.softmax(jnp.concatenate([s_past, s_new], axis=-1), axis=-1)
    P = k_past.shape[1]
    o = (jnp.einsum("bhqk,bkhd->bqhd", p[..., :P].astype(v_past.dtype), v_past)
         + jnp.einsum("bhqk,bkhd->bqhd", p[..., P:].astype(v.dtype), v))
    return o.reshape(DB, T, D_MODEL).astype(xn.dtype) @ w_out, k, v, logf


def gla_chunked(q, k, v, g, s0, chunk):
    B, L = q.shape[:2]
    n = L // chunk
    to_chunks = lambda a: jnp.moveaxis(a.reshape(B, n, chunk, *a.shape[2:]), 1, 0)
    tri = jnp.tril(jnp.ones((chunk, chunk), dtype=bool))[None, :, :, None, None]

    def step(S, xs):
        qc, kc, vc, gc = (a.astype(jnp.float32) for a in xs)
        G = jnp.cumsum(gc, axis=1)
        decay = jnp.exp(jnp.where(tri, G[:, :, None] - G[:, None, :], -jnp.inf))
        A = jnp.einsum("bthk,btshk,bshk->bhts", qc, decay, kc)
        o = (jnp.einsum("bhts,bshv->bthv", A, vc)
             + jnp.einsum("bthk,bhkv->bthv", qc * jnp.exp(G), S))
        g_last = G[:, -1]
        S_new = (jnp.exp(g_last)[..., None] * S
                 + jnp.einsum("bshk,bshv->bhkv", kc * jnp.exp(g_last[:, None] - G), vc))
        return S_new, o

    s_fin, o = lax.scan(step, s0, (to_chunks(q), to_chunks(k), to_chunks(v), to_chunks(g)))
    o = jnp.moveaxis(o, 0, 1).reshape(B, L, q.shape[2], v.shape[-1])
    return o, s_fin


def gla_mixer(xn, s0, chunk, w_in, w_gate2, b_gate, norm_g, w_out):
    proj = xn @ w_in
    lead = xn.shape[:-1]
    o1, o2, o3, o4 = GLA_DK, 2 * GLA_DK, 2 * GLA_DK + GLA_DV, 2 * GLA_DK + 2 * GLA_DV
    q = proj[..., :o1].reshape(*lead, GLA_HEADS, GLA_DK_H) * GLA_SCALE
    k = proj[..., o1:o2].reshape(*lead, GLA_HEADS, GLA_DK_H)
    v = proj[..., o2:o3].reshape(*lead, GLA_HEADS, GLA_DV_H)
    r = proj[..., o3:o4]
    g = jax.nn.log_sigmoid((proj[..., o4:] @ w_gate2 + b_gate).astype(jnp.float32)) / GLA_TAU
    g = g.reshape(*lead, GLA_HEADS, GLA_DK_H)
    o, s_fin = gla_chunked(q, k, v, g, s0.astype(jnp.float32), chunk)
    o = rmsnorm(o, norm_g.reshape(GLA_HEADS, GLA_DV_H)).reshape(*lead, GLA_DV).astype(xn.dtype)
    return (o * jax.nn.silu(r)) @ w_out, s_fin


def setup_inputs(seed: int = 0) -> dict:
    key = jax.random.key(seed)
    ks = jax.random.split(key, 20)
    n_pages = PAST_LEN // PAGE_SIZE
    n_pool = (5 * DEC_BATCH * n_pages) // 4
    f32 = jnp.float32
    nrm = lambda k, shape, s: jax.random.normal(k, shape, f32) * s
    page_table = jax.random.permutation(ks[0], n_pool)[:DEC_BATCH * n_pages]
    page_table = page_table.reshape(DEC_BATCH, n_pages).astype(jnp.int32)
    gla_in_w = 2 * GLA_DK + 2 * GLA_DV + GLA_GATE_RANK
    return {
        "x_prompt": nrm(ks[1], (BATCH, SEQ, D_MODEL), 1.0),
        "x_sample": nrm(ks[2], (DEC_BATCH, DEC_SEQ, D_MODEL), 1.0),
        "cache_k": nrm(ks[3], (N_FOX, n_pool, PAGE_SIZE, FOX_HEADS, FOX_HEAD_DIM), 1.0),
        "cache_v": nrm(ks[4], (N_FOX, n_pool, PAGE_SIZE, FOX_HEADS, FOX_HEAD_DIM), 1.0),
        "cache_logf": jax.nn.log_sigmoid(2.0 + nrm(ks[5], (N_FOX, n_pool, PAGE_SIZE, FOX_HEADS), 1.0)),
        "state_gla": nrm(ks[6], (N_GLA, DEC_BATCH, GLA_HEADS, GLA_DK_H, GLA_DV_H), 0.5),
        "page_table": page_table,
        "meta": nrm(ks[7], (N_META, D_MODEL), 1.0),
        "norm_g": 1.0 + nrm(ks[8], (DEPTH, 4, D_MODEL), 0.05),
        "fox_w_in": nrm(ks[9], (N_FOX, D_MODEL, 3 * D_MODEL + FOX_HEADS), D_MODEL ** -0.5),
        "fox_b_f": 1.0 + nrm(ks[10], (N_FOX, FOX_HEADS), 0.1),
        "fox_w_out": nrm(ks[11], (N_FOX, D_MODEL, D_MODEL), D_MODEL ** -0.5),
        "gla_w_in": nrm(ks[12], (N_GLA, D_MODEL, gla_in_w), D_MODEL ** -0.5),
        "gla_w_gate2": nrm(ks[13], (N_GLA, GLA_GATE_RANK, GLA_DK), GLA_GATE_RANK ** -0.5),
        "gla_b_gate": nrm(ks[14], (N_GLA, GLA_DK), 0.1),
        "gla_norm_g": 1.0 + nrm(ks[15], (N_GLA, GLA_DV), 0.05),
        "gla_w_out": nrm(ks[16], (N_GLA, GLA_DV, D_MODEL), GLA_DV ** -0.5),
        "ffn_w_up": nrm(ks[17], (DEPTH, D_MODEL, 2 * D_FF), D_MODEL ** -0.5),
        "ffn_w_down": nrm(ks[18], (DEPTH, D_FF, D_MODEL), D_FF ** -0.5),
    }


def reference(x_prompt, x_sample, cache_k, cache_v, cache_logf, state_gla, page_table,
              meta, norm_g, fox_w_in, fox_b_f, fox_w_out,
              gla_w_in, gla_w_gate2, gla_b_gate, gla_norm_g, gla_w_out,
              ffn_w_up, ffn_w_down):
    B = x_prompt.shape[0]
    DB, T = x_sample.shape[:2]
    dt = x_prompt.dtype
    xp = jnp.concatenate([jnp.zeros((B, META_PAD, D_MODEL), dt),
                          jnp.broadcast_to(meta.astype(dt)[None], (B, N_META, D_MODEL)),
                          x_prompt], axis=1)
    L = xp.shape[1]
    valid = jnp.arange(L) >= META_PAD
    vmask = valid[None, :, None].astype(dt)
    xs = x_sample

    nk_p, nv_p, nf_p, ns_p = [], [], [], []
    nk_s, nv_s, nf_s, ns_s = [], [], [], []
    for i in range(DEPTH):
        g = norm_g[i]
        j = i // N_MIXERS
        xpn, xsn = rmsnorm(xp, g[0]), rmsnorm(xs, g[0])
        if i % N_MIXERS == 0:
            mp, kp, vp, lfp = fox_prompt(xpn, valid, fox_w_in[j], fox_b_f[j], fox_w_out[j])
            k_past = cache_k[j][page_table].reshape(DB, -1, FOX_HEADS, FOX_HEAD_DIM)
            v_past = cache_v[j][page_table].reshape(DB, -1, FOX_HEADS, FOX_HEAD_DIM)
            lf_past = cache_logf[j][page_table].reshape(DB, -1, FOX_HEADS)
            ms, ksn, vsn, lfs = fox_sample(xsn, k_past, v_past, lf_past,
                                           fox_w_in[j], fox_b_f[j], fox_w_out[j])
            nk_p.append(kp[:, META_PAD:]); nv_p.append(vp[:, META_PAD:]); nf_p.append(lfp[:, META_PAD:])
            nk_s.append(ksn); nv_s.append(vsn); nf_s.append(lfs)
        else:
            s0 = jnp.zeros((B, GLA_HEADS, GLA_DK_H, GLA_DV_H), jnp.float32)
            mp, sp = gla_mixer(xpn, s0, GLA_CHUNK, gla_w_in[j], gla_w_gate2[j], gla_b_gate[j],
                               gla_norm_g[j], gla_w_out[j])
            ms, ss = gla_mixer(xsn, state_gla[j], T, gla_w_in[j], gla_w_gate2[j], gla_b_gate[j],
                               gla_norm_g[j], gla_w_out[j])
            ns_p.append(sp); ns_s.append(ss)
        xp = xp + vmask * rmsnorm(mp.astype(dt), g[1])
        xs = xs + rmsnorm(ms.astype(xs.dtype), g[1])
        xp = xp + rmsnorm(swiglu(rmsnorm(xp, g[2]), ffn_w_up[i], ffn_w_down[i]), g[3])
        xs = xs + rmsnorm(swiglu(rmsnorm(xs, g[2]), ffn_w_up[i], ffn_w_down[i]), g[3])

    y_prompt = xp[:, META_BLOCK:]
    y_sample = xs
    return (y_prompt, y_sample,
            jnp.stack(nk_p), jnp.stack(nv_p), jnp.stack(nf_p), jnp.stack(ns_p),
            jnp.stack(nk_s), jnp.stack(nv_s), jnp.stack(nf_s), jnp.stack(ns_s))
```

```python
import functools

import jax
import jax.numpy as jnp
from jax import lax
from jax.experimental import pallas as pl
from jax.experimental.pallas import tpu as pltpu

D_MODEL = 1024
N_META_TOK = 16
FOX_HEADS = 16
FOX_HEAD_DIM = 64
FOX_SCALE = FOX_HEAD_DIM ** -0.5
GLA_HEADS = 4
GLA_DK = 512
GLA_DV = 1024
GLA_DK_H = 128
GLA_DV_H = 256
GLA_SCALE = GLA_DK_H ** -0.5
GLA_RANK = 16
GLA_TAU = 16.0
D_FF = 2816
RMS_EPS = 1e-6
NEG = -1e30
PAGE = 128

LANES = 128
FF_CHUNK = 256
N_FF_CHUNKS = D_FF // FF_CHUNK
GLA_CHUNK = 128
GLA_SUB = 16
EXP_CLAMP = 80.0
VMEM_LIMIT = 56 * 1024 * 1024

F32 = jnp.float32
BF16 = jnp.bfloat16
NT = (((1,), (1,)), ((), ()))
TN = (((0,), (0,)), ((), ()))


def _cparams(sem):
    return pltpu.CompilerParams(dimension_semantics=sem, vmem_limit_bytes=VMEM_LIMIT)


def _rms(x, g):
    return x * lax.rsqrt(jnp.mean(x * x, axis=-1, keepdims=True) + RMS_EPS) * g


def _dot(a, b):
    return jnp.dot(a, b, preferred_element_type=F32)


def _select_dot(sel, x, dims=None, sel_first=True):
    hi = x.astype(BF16)
    r1 = x - hi.astype(F32)
    mid = r1.astype(BF16)
    lo = (r1 - mid.astype(F32)).astype(BF16)
    sel = sel.astype(BF16)
    out = None
    for piece in (hi, mid, lo):
        a, b = (sel, piece) if sel_first else (piece, sel)
        d = _dot(a, b) if dims is None else lax.dot_general(a, b, dims, preferred_element_type=F32)
        out = d if out is None else out + d
    return out


def _log_sigmoid(z):
    return jnp.minimum(z, 0.0) - jnp.log1p(jnp.exp(-jnp.abs(z)))


def _const_spec(shape):
    n = len(shape)
    return pl.BlockSpec(shape, lambda *_: (0,) * n, pipeline_mode=pl.Buffered(1))


def _row_spec(tm, width):
    return pl.BlockSpec((1, tm, width), lambda b, i: (b, i, 0))


def _fox_proj_kernel(x_ref, g_ref, wq_ref, wk_ref, wv_ref, wf_ref, bf_ref,
                     q16_ref, k_ref, v_ref, k16_ref, v16_ref, lf_ref):
    xn = _rms(x_ref[0], g_ref[...]).astype(BF16)
    q16_ref[0] = (_dot(xn, wq_ref[...]) * FOX_SCALE).astype(BF16)
    k = _dot(xn, wk_ref[...])
    k_ref[0] = k
    k16_ref[0] = k.astype(BF16)
    v = _dot(xn, wv_ref[...])
    v_ref[0] = v
    v16_ref[0] = v.astype(BF16)
    z = _dot(xn, wf_ref[...])[:, :FOX_HEADS] + bf_ref[...]
    lf_ref[0] = _log_sigmoid(z)


def _fox_proj(x, g, wq, wk, wv, wf, bf, tm):
    nb, length, _ = x.shape
    sds = jax.ShapeDtypeStruct
    return pl.pallas_call(
        _fox_proj_kernel,
        grid=(nb, length // tm),
        in_specs=[_row_spec(tm, D_MODEL), _const_spec((1, D_MODEL)),
                  _const_spec((D_MODEL, D_MODEL)), _const_spec((D_MODEL, D_MODEL)),
                  _const_spec((D_MODEL, D_MODEL)), _const_spec((D_MODEL, LANES)),
                  _const_spec((1, FOX_HEADS))],
        out_specs=[_row_spec(tm, D_MODEL)] * 5 + [_row_spec(tm, FOX_HEADS)],
        out_shape=[sds((nb, length, D_MODEL), BF16), sds((nb, length, D_MODEL), F32),
                   sds((nb, length, D_MODEL), F32), sds((nb, length, D_MODEL), BF16),
                   sds((nb, length, D_MODEL), BF16), sds((nb, length, FOX_HEADS), F32)],
        compiler_params=_cparams(("parallel", "parallel")),
        name="fox_proj",
    )(x, g, wq, wk, wv, wf, bf)


def _fox_cumsum_kernel(nblk, lf_ref, ccol_ref, crow_ref):
    r = lax.broadcasted_iota(jnp.int32, (LANES, LANES), 0)
    c = lax.broadcasted_iota(jnp.int32, (LANES, LANES), 1)
    tri = (c <= r).astype(F32)
    er = lax.broadcasted_iota(jnp.int32, (FOX_HEADS, FOX_HEADS), 0)
    ec = lax.broadcasted_iota(jnp.int32, (FOX_HEADS, FOX_HEADS), 1)
    eye = (er == ec).astype(F32)
    carry = jnp.zeros((1, FOX_HEADS), F32)
    for b in range(nblk):
        rows = slice(b * LANES, (b + 1) * LANES)
        cb = _select_dot(tri, lf_ref[0, rows, :]) + carry
        carry = cb[LANES - 1:LANES, :]
        ccol_ref[0, rows, :] = cb
        crow_ref[0, :, rows] = _select_dot(eye, cb, NT)


def _fox_cumsum(lf):
    nb, length, _ = lf.shape
    sds = jax.ShapeDtypeStruct
    return pl.pallas_call(
        functools.partial(_fox_cumsum_kernel, length // LANES),
        grid=(nb,),
        in_specs=[pl.BlockSpec((1, length, FOX_HEADS), lambda b: (b, 0, 0))],
        out_specs=[pl.BlockSpec((1, length, FOX_HEADS), lambda b: (b, 0, 0)),
                   pl.BlockSpec((1, FOX_HEADS, length), lambda b: (b, 0, 0))],
        out_shape=[sds((nb, length, FOX_HEADS), F32), sds((nb, FOX_HEADS, length), F32)],
        compiler_params=_cparams(("parallel",)),
        name="fox_cumsum",
    )(lf)


def _fox_attn_kernel(tq, q_ref, k_ref, v_ref, ccol_ref, crow_ref, o_ref):
    hp = pl.program_id(1)
    qi = pl.program_id(2)
    q2 = q_ref[0]
    lane = lax.broadcasted_iota(jnp.int32, (tq, LANES), 1)
    first = lane < FOX_HEAD_DIM
    zero = jnp.zeros_like(q2)
    qs = (jnp.where(first, q2, zero), jnp.where(first, zero, q2))
    cc = ccol_ref[0]
    lane16 = lax.broadcasted_iota(jnp.int32, (tq, FOX_HEADS), 1)
    cq = tuple(jnp.sum(jnp.where(lane16 == 2 * hp + i, cc, 0.0), axis=-1, keepdims=True) for i in range(2))
    row0 = qi * tq
    qpos = row0 + lax.broadcasted_iota(jnp.int32, (tq, LANES), 0)
    n_kv = (row0 + tq + LANES - 1) // LANES
    n_full = (row0 + 1) // LANES

    def make_step(masked):
        def step(j, carry):
            off = pl.multiple_of(j * LANES, LANES)
            kb = k_ref[0, pl.ds(off, LANES), :]
            vb = v_ref[0, pl.ds(off, LANES), :]
            if masked:
                keep = (off + lane) <= qpos
            out = []
            for i in range(2):
                m, l, acc = carry[i]
                ck = crow_ref[0, 0, i:i + 1, pl.ds(off, LANES)]
                s = lax.dot_general(qs[i], kb, NT, preferred_element_type=F32) + (cq[i] - ck)
                if masked:
                    s = jnp.where(keep, s, NEG)
                m_new = jnp.maximum(m, jnp.max(s, axis=-1, keepdims=True))
                alpha = jnp.exp(m - m_new)
                p = jnp.exp(s - m_new)
                l = alpha * l + jnp.sum(p, axis=-1, keepdims=True)
                acc = alpha * acc + _dot(p.astype(BF16), vb)
                out.append((m_new, l, acc))
            return tuple(out)
        return step

    init = tuple((jnp.full((tq, 1), NEG, F32), jnp.zeros((tq, 1), F32), jnp.zeros((tq, LANES), F32))
                 for _ in range(2))
    carry = lax.fori_loop(0, n_full, make_step(False), init)
    carry = lax.fori_loop(n_full, n_kv, make_step(True), carry)
    (_, la, acca), (_, lb, accb) = carry
    o_ref[0] = jnp.where(first, acca / la, accb / lb).astype(BF16)


def _fox_attn(q16, k16, v16, ccol, crow, tq):
    nb, length, _ = q16.shape
    n_hp = FOX_HEADS // 2
    crow4 = crow.reshape(nb, n_hp, 2, length)
    return pl.pallas_call(
        functools.partial(_fox_attn_kernel, tq),
        grid=(nb, n_hp, length // tq),
        in_specs=[pl.BlockSpec((1, tq, LANES), lambda b, h, i: (b, i, h)),
                  pl.BlockSpec((1, length, LANES), lambda b, h, i: (b, 0, h)),
                  pl.BlockSpec((1, length, LANES), lambda b, h, i: (b, 0, h)),
                  pl.BlockSpec((1, tq, FOX_HEADS), lambda b, h, i: (b, i, 0)),
                  pl.BlockSpec((1, 1, 2, length), lambda b, h, i: (b, h, 0, 0))],
        out_specs=pl.BlockSpec((1, tq, LANES), lambda b, h, i: (b, i, h)),
        out_shape=jax.ShapeDtypeStruct((nb, length, D_MODEL), BF16),
        compiler_params=_cparams(("parallel", "parallel", "parallel")),
        name="fox_attn_prompt",
    )(q16, k16, v16, ccol, crow4)


def _fox_new_bias_kernel(t_new, lf_ref, nt_ref):
    n = lf_ref.shape[0]
    r = lax.broadcasted_iota(jnp.int32, (n, n), 0)
    c = lax.broadcasted_iota(jnp.int32, (n, n), 1)
    same_seq = (r // t_new) == (c // t_new)
    bd = (same_seq & (c <= r)).astype(F32)
    cum = _select_dot(bd, lf_ref[...])
    er = lax.broadcasted_iota(jnp.int32, (FOX_HEADS, FOX_HEADS), 0)
    ec = lax.broadcasted_iota(jnp.int32, (FOX_HEADS, FOX_HEADS), 1)
    eye = (er == ec).astype(F32)
    nt_ref[...] = _select_dot(eye, cum, NT)


def _fox_new_bias(lf, t_new):
    n = lf.shape[0]
    return pl.pallas_call(
        functools.partial(_fox_new_bias_kernel, t_new),
        out_shape=jax.ShapeDtypeStruct((FOX_HEADS, n), F32),
        compiler_params=pltpu.CompilerParams(vmem_limit_bytes=VMEM_LIMIT),
        name="fox_new_bias",
    )(lf)


def _fox_sample_kernel(t_new, n_pages, pt_ref, q_ref, kn_ref, vn_ref, nt_ref, kp_ref, vp_ref, lf_ref,
                       o_ref, qexp_ref, m_ref, l_ref, acc_ref, carry_ref):
    del pt_ref
    step = pl.program_id(1)
    rows = t_new * FOX_HEADS
    row = lax.broadcasted_iota(jnp.int32, (rows, D_MODEL), 0)
    lane = lax.broadcasted_iota(jnp.int32, (rows, D_MODEL), 1)
    own_head = (row % FOX_HEADS) == (lane // FOX_HEAD_DIM)

    def update(s, v_blk):
        m = m_ref[...]
        m_new = jnp.maximum(m, jnp.max(s, axis=-1, keepdims=True))
        alpha = jnp.exp(m - m_new)
        p = jnp.exp(s - m_new)
        l_ref[...] = alpha * l_ref[...] + jnp.sum(p, axis=-1, keepdims=True)
        acc_ref[...] = alpha * acc_ref[...] + _dot(p.astype(v_blk.dtype), v_blk)
        m_ref[...] = m_new

    @pl.when(step == 0)
    def _():
        q = q_ref[0]
        qrep = jnp.concatenate([jnp.broadcast_to(q[t:t + 1, :], (FOX_HEADS, D_MODEL)) for t in range(t_new)], axis=0)
        qexp = jnp.where(own_head, qrep, 0.0).astype(BF16)
        qexp_ref[...] = qexp
        carry_ref[...] = jnp.zeros(carry_ref.shape, F32)
        qf = qexp.astype(F32)
        kn = kn_ref[0].astype(BF16).astype(F32)
        vn = vn_ref[0]
        nt = nt_ref[0]
        row_t = lax.broadcasted_iota(jnp.int32, (rows, 1), 0) // FOX_HEADS
        cols = []
        for t2 in range(t_new):
            sc = jnp.sum(qf * kn[t2:t2 + 1, :], axis=-1, keepdims=True)
            sc = sc - jnp.concatenate([nt[:, t2:t2 + 1]] * t_new, axis=0)
            cols.append(jnp.where(row_t >= t2, sc, NEG))
        m = functools.reduce(jnp.maximum, cols)
        ps = [jnp.exp(sc - m) for sc in cols]
        m_ref[...] = m
        l_ref[...] = functools.reduce(jnp.add, ps)
        acc_ref[...] = functools.reduce(jnp.add, [ps[t2] * vn[t2:t2 + 1, :] for t2 in range(t_new)])

    @pl.when(step > 0)
    def _():
        lf = lf_ref[...]
        j = lax.broadcasted_iota(jnp.int32, (PAGE, PAGE), 0)
        key = lax.broadcasted_iota(jnp.int32, (PAGE, PAGE), 1)
        later = (j > key).astype(F32)
        suf = _select_dot(later, lf, sel_first=False) + carry_ref[...]
        carry_ref[...] = suf[:, 0:1] + lf[:, 0:1]
        s = lax.dot_general(qexp_ref[...], kp_ref[...].astype(BF16), NT, preferred_element_type=F32)
        s = s + jnp.concatenate([suf] * t_new, axis=0)
        update(s, vp_ref[...].astype(BF16))

    @pl.when(step == n_pages)
    def _():
        o = jnp.where(own_head, acc_ref[...] / l_ref[...], 0.0)
        o_ref[0, 0:t_new, :] = o.reshape(t_new, FOX_HEADS, D_MODEL).sum(axis=1)
        o_ref[0, t_new:, :] = jnp.zeros((o_ref.shape[1] - t_new, D_MODEL), F32)


def _fox_sample(layer, t_new, page_table, q_new, k_new, v_new, nt, cache_k4, cache_v4, cache_lft):
    n_seq, tpad, _ = q_new.shape
    n_pages = page_table.shape[1]
    rows = t_new * FOX_HEADS

    def page_idx(b, s, pt):
        return pt[b * n_pages + (n_pages - jnp.maximum(s, 1))]

    grid_spec = pltpu.PrefetchScalarGridSpec(
        num_scalar_prefetch=1,
        grid=(n_seq, n_pages + 1),
        in_specs=[pl.BlockSpec((1, tpad, D_MODEL), lambda b, s, pt: (b, 0, 0)),
                  pl.BlockSpec((1, tpad, D_MODEL), lambda b, s, pt: (b, 0, 0)),
                  pl.BlockSpec((1, tpad, D_MODEL), lambda b, s, pt: (b, 0, 0)),
                  pl.BlockSpec((1, FOX_HEADS, tpad), lambda b, s, pt: (b, 0, 0)),
                  pl.BlockSpec((None, None, PAGE, D_MODEL), lambda b, s, pt: (layer, page_idx(b, s, pt), 0, 0)),
                  pl.BlockSpec((None, None, PAGE, D_MODEL), lambda b, s, pt: (layer, page_idx(b, s, pt), 0, 0)),
                  pl.BlockSpec((None, None, FOX_HEADS, PAGE), lambda b, s, pt: (layer, page_idx(b, s, pt), 0, 0))],
        out_specs=pl.BlockSpec((1, tpad, D_MODEL), lambda b, s, pt: (b, 0, 0)),
        scratch_shapes=[pltpu.VMEM((rows, D_MODEL), BF16), pltpu.VMEM((rows, 1), F32), pltpu.VMEM((rows, 1), F32),
                        pltpu.VMEM((rows, D_MODEL), F32), pltpu.VMEM((FOX_HEADS, 1), F32)])
    return pl.pallas_call(
        functools.partial(_fox_sample_kernel, t_new, n_pages),
        grid_spec=grid_spec,
        out_shape=jax.ShapeDtypeStruct((n_seq, tpad, D_MODEL), F32),
        compiler_params=_cparams(("parallel", "arbitrary")),
        name="fox_attn_sample",
    )(page_table.reshape(-1), q_new, k_new, v_new, nt, cache_k4, cache_v4, cache_lft)


def _valid_rows(tm, n_valid):
    row = pl.program_id(1) * tm + lax.broadcasted_iota(jnp.int32, (tm, 1), 0)
    return row < n_valid


def _fox_out_kernel(tm, n_valid, x_ref, o_ref, w_ref, g_ref, y_ref):
    m = _dot(o_ref[0].astype(BF16), w_ref[...])
    y_ref[0] = x_ref[0] + jnp.where(_valid_rows(tm, n_valid), _rms(m, g_ref[...]), 0.0)


def _fox_out(x, o, w, g, tm, n_valid):
    nb, length, _ = x.shape
    return pl.pallas_call(
        functools.partial(_fox_out_kernel, tm, n_valid),
        grid=(nb, length // tm),
        in_specs=[_row_spec(tm, D_MODEL), _row_spec(tm, D_MODEL),
                  _const_spec((D_MODEL, D_MODEL)), _const_spec((1, D_MODEL))],
        out_specs=_row_spec(tm, D_MODEL),
        out_shape=jax.ShapeDtypeStruct(x.shape, F32),
        compiler_params=_cparams(("parallel", "parallel")),
        name="fox_out",
    )(x, o, w, g)


def _gla_out_kernel(tm, n_valid, x_ref, o_ref, r_ref, gn_ref, w_ref, g_ref, y_ref):
    o = o_ref[0]
    r = r_ref[0]
    gn = gn_ref[...]
    parts = []
    for h in range(GLA_HEADS):
        cols = slice(h * GLA_DV_H, (h + 1) * GLA_DV_H)
        on = _rms(o[:, cols], gn[:, cols])
        rh = r[:, cols]
        parts.append((on * (rh * jax.nn.sigmoid(rh))).astype(BF16))
    m = _dot(jnp.concatenate(parts, axis=1), w_ref[...])
    y_ref[0] = x_ref[0] + jnp.where(_valid_rows(tm, n_valid), _rms(m, g_ref[...]), 0.0)


def _gla_out(x, o, r, gn, w, g, tm, n_valid):
    nb, length, _ = x.shape
    return pl.pallas_call(
        functools.partial(_gla_out_kernel, tm, n_valid),
        grid=(nb, length // tm),
        in_specs=[_row_spec(tm, D_MODEL), _row_spec(tm, GLA_DV), _row_spec(tm, GLA_DV),
                  _const_spec((1, GLA_DV)), _const_spec((GLA_DV, D_MODEL)), _const_spec((1, D_MODEL))],
        out_specs=_row_spec(tm, D_MODEL),
        out_shape=jax.ShapeDtypeStruct(x.shape, F32),
        compiler_params=_cparams(("parallel", "parallel")),
        name="gla_out",
    )(x, o, r, gn, w, g)


def _ffn_kernel(x_ref, g2_ref, g3_ref, wup_ref, wdn_ref, y_ref, xn_ref, acc_ref):
    x = x_ref[0]
    xn_ref[...] = _rms(x, g2_ref[...]).astype(BF16)
    acc_ref[...] = jnp.zeros(acc_ref.shape, F32)

    def chunk(c, _):
        xn = xn_ref[...]
        h1 = _dot(xn, wup_ref[c])
        h2 = _dot(xn, wup_ref[c + N_FF_CHUNKS])
        a = (h1 * jax.nn.sigmoid(h1) * h2).astype(BF16)
        acc_ref[...] += _dot(a, wdn_ref[c])
        return 0

    lax.fori_loop(0, N_FF_CHUNKS, chunk, 0)
    y_ref[0] = x + _rms(acc_ref[...], g3_ref[...])


def _ffn(x, g2, g3, wup, wdn, tm):
    nb, length, _ = x.shape
    return pl.pallas_call(
        _ffn_kernel,
        grid=(nb, length // tm),
        in_specs=[_row_spec(tm, D_MODEL), _const_spec((1, D_MODEL)), _const_spec((1, D_MODEL)),
                  _const_spec((2 * N_FF_CHUNKS, D_MODEL, FF_CHUNK)), _const_spec((N_FF_CHUNKS, FF_CHUNK, D_MODEL))],
        out_specs=_row_spec(tm, D_MODEL),
        out_shape=jax.ShapeDtypeStruct(x.shape, F32),
        scratch_shapes=[pltpu.VMEM((tm, D_MODEL), BF16), pltpu.VMEM((tm, D_MODEL), F32)],
        compiler_params=_cparams(("parallel", "parallel")),
        name="ffn",
    )(x, g2, g3, wup, wdn)


def _gla_proj_kernel(x_ref, g_ref, wq_ref, wk_ref, wv_ref, wr_ref, wl_ref, wg2_ref, bg_ref,
                     q_ref, k_ref, v_ref, r_ref, gate_ref):
    xn = _rms(x_ref[0], g_ref[...]).astype(BF16)
    q_ref[0] = _dot(xn, wq_ref[...]) * GLA_SCALE
    k_ref[0] = _dot(xn, wk_ref[...])
    v_ref[0] = _dot(xn, wv_ref[...]).astype(BF16)
    r_ref[0] = _dot(xn, wr_ref[...])
    low = _dot(xn, wl_ref[...]).astype(BF16)
    z = _dot(low, wg2_ref[...]) + bg_ref[...]
    gate_ref[0] = _log_sigmoid(z) * (1.0 / GLA_TAU)


def _gla_proj(x, g, wq, wk, wv, wr, wl, wg2, bg, tm):
    nb, length, _ = x.shape
    sds = jax.ShapeDtypeStruct
    return pl.pallas_call(
        _gla_proj_kernel,
        grid=(nb, length // tm),
        in_specs=[_row_spec(tm, D_MODEL), _const_spec((1, D_MODEL)),
                  _const_spec((D_MODEL, GLA_DK)), _const_spec((D_MODEL, GLA_DK)),
                  _const_spec((D_MODEL, GLA_DV)), _const_spec((D_MODEL, GLA_DV)),
                  _const_spec((D_MODEL, LANES)), _const_spec((LANES, GLA_DK)), _const_spec((1, GLA_DK))],
        out_specs=[_row_spec(tm, GLA_DK), _row_spec(tm, GLA_DK), _row_spec(tm, GLA_DV),
                   _row_spec(tm, GLA_DV), _row_spec(tm, GLA_DK)],
        out_shape=[sds((nb, length, GLA_DK), F32), sds((nb, length, GLA_DK), F32),
                   sds((nb, length, GLA_DV), BF16), sds((nb, length, GLA_DV), F32),
                   sds((nb, length, GLA_DK), F32)],
        compiler_params=_cparams(("parallel", "parallel")),
        name="gla_proj",
    )(x, g, wq, wk, wv, wr, wl, wg2, bg)


def _gla_prompt_kernel(n_chunks, n_valid, q_ref, k_ref, g_ref, v_ref, o_ref, s_out_ref, s_ref):
    s_ref[...] = jnp.zeros(s_ref.shape, F32)
    r = lax.broadcasted_iota(jnp.int32, (GLA_CHUNK, GLA_CHUNK), 0)
    c = lax.broadcasted_iota(jnp.int32, (GLA_CHUNK, GLA_CHUNK), 1)
    causal = c <= r
    tri = causal.astype(F32)
    n_sub = GLA_CHUNK // GLA_SUB

    def chunk(ci, _):
        off = pl.multiple_of(ci * GLA_CHUNK, GLA_CHUNK)
        rows = pl.ds(off, GLA_CHUNK)
        pos = off + lax.broadcasted_iota(jnp.int32, (GLA_CHUNK, 1), 0)
        g = jnp.where(pos < n_valid, g_ref[0, rows, :], 0.0)
        q = q_ref[0, rows, :]
        k = k_ref[0, rows, :]
        v = v_ref[0, rows, :]
        big_g = _select_dot(tri, g)
        g_last = big_g[GLA_CHUNK - 1:GLA_CHUNK, :]
        g_last_col = big_g.T[:, GLA_CHUNK - 1:GLA_CHUNK]
        state = s_ref[...]
        o = _dot((q * jnp.exp(big_g)).astype(BF16), state.astype(BF16))
        a_parts = []
        for i in range(n_sub):
            lo, hi = i * GLA_SUB, (i + 1) * GLA_SUB
            ref = big_g[lo - 1:lo, :] if i > 0 else jnp.zeros((1, GLA_DK_H), F32)
            q_i = (q[lo:hi, :] * jnp.exp(big_g[lo:hi, :] - ref)).astype(BF16)
            e = jnp.where(r < hi, jnp.exp(jnp.minimum(ref - big_g, EXP_CLAMP)), 0.0)
            a_parts.append(lax.dot_general(q_i, (k * e).astype(BF16), NT, preferred_element_type=F32))
        a = jnp.where(causal, jnp.concatenate(a_parts, axis=0), 0.0)
        o_ref[0, rows, :] = o + _dot(a.astype(BF16), v)
        k_dec = (k * jnp.exp(g_last - big_g)).T.astype(BF16)
        s_ref[...] = state * jnp.exp(g_last_col) + _dot(k_dec, v)
        return 0

    lax.fori_loop(0, n_chunks, chunk, 0)
    s_out_ref[0, 0] = s_ref[...]


def _gla_prompt(q, k, g, v, n_valid):
    nb, length, _ = q.shape
    sds = jax.ShapeDtypeStruct
    kspec = pl.BlockSpec((1, length, GLA_DK_H), lambda b, h: (b, 0, h))
    vspec = pl.BlockSpec((1, length, GLA_DV_H), lambda b, h: (b, 0, h))
    return pl.pallas_call(
        functools.partial(_gla_prompt_kernel, length // GLA_CHUNK, n_valid),
        grid=(nb, GLA_HEADS),
        in_specs=[kspec, kspec, kspec, vspec],
        out_specs=[vspec, pl.BlockSpec((1, 1, GLA_DK_H, GLA_DV_H), lambda b, h: (b, h, 0, 0))],
        out_shape=[sds((nb, length, GLA_DV), F32), sds((nb, GLA_HEADS, GLA_DK_H, GLA_DV_H), F32)],
        scratch_shapes=[pltpu.VMEM((GLA_DK_H, GLA_DV_H), F32)],
        compiler_params=_cparams(("parallel", "parallel")),
        name="gla_prompt",
    )(q, k, g, v)


def _gla_sample_kernel(sb, t_new, q_ref, k_ref, g_ref, v_ref, s_in_ref, o_ref, s_out_ref):
    tpad = q_ref.shape[1]
    rowi = lax.broadcasted_iota(jnp.int32, (tpad, GLA_DK_H), 0)
    rowc = lax.broadcasted_iota(jnp.int32, (tpad, 1), 0)
    for i in range(sb):
        for h in range(GLA_HEADS):
            kcols = slice(h * GLA_DK_H, (h + 1) * GLA_DK_H)
            vcols = slice(h * GLA_DV_H, (h + 1) * GLA_DV_H)
            g = g_ref[i, :, kcols]
            q = q_ref[i, :, kcols]
            k = k_ref[i, :, kcols]
            v = v_ref[i, :, vcols]
            big_g = jnp.zeros((tpad, GLA_DK_H), F32)
            for u in range(t_new):
                big_g = big_g + jnp.where(rowi >= u, g[u:u + 1, :], 0.0)
            state = s_in_ref[0, i, h]
            o = _dot(q * jnp.exp(big_g), state)
            for s in range(t_new):
                w = jnp.exp(jnp.where(rowi >= s, big_g - big_g[s:s + 1, :], 0.0))
                a = jnp.sum(q * k[s:s + 1, :] * w, axis=-1, keepdims=True)
                o = o + jnp.where(rowc >= s, a, 0.0) * v[s:s + 1, :]
            o_ref[i, :, vcols] = o
            g_last = big_g[t_new - 1:t_new, :]
            k_dec = k * jnp.exp(g_last - big_g)
            g_last_col = jnp.broadcast_to(g_last, (tpad, GLA_DK_H)).T[:, 0:1]
            s_out_ref[0, i, h] = state * jnp.exp(g_last_col) + lax.dot_general(k_dec, v, TN, preferred_element_type=F32)


def _gla_sample(layer, t_new, q, k, g, v, state_all, sb):
    n_seq, tpad, _ = q.shape
    sds = jax.ShapeDtypeStruct
    kspec = pl.BlockSpec((sb, tpad, GLA_DK), lambda b: (b, 0, 0))
    vspec = pl.BlockSpec((sb, tpad, GLA_DV), lambda b: (b, 0, 0))
    sspec_in = pl.BlockSpec((1, sb, GLA_HEADS, GLA_DK_H, GLA_DV_H), lambda b: (layer, b, 0, 0, 0))
    sspec_out = pl.BlockSpec((1, sb, GLA_HEADS, GLA_DK_H, GLA_DV_H), lambda b: (0, b, 0, 0, 0))
    o, s_new = pl.pallas_call(
        functools.partial(_gla_sample_kernel, sb, t_new),
        grid=(n_seq // sb,),
        in_specs=[kspec, kspec, kspec, vspec, sspec_in],
        out_specs=[vspec, sspec_out],
        out_shape=[sds((n_seq, tpad, GLA_DV), F32), sds((1, n_seq, GLA_HEADS, GLA_DK_H, GLA_DV_H), F32)],
        compiler_params=_cparams(("parallel",)),
        name="gla_sample",
    )(q, k, g, v, state_all)
    return o, s_new[0]


def kernel(x_prompt, x_sample, cache_k, cache_v, cache_logf, state_gla, page_table, meta, norm_g, fox_w_in, fox_b_f, fox_w_out, gla_w_in, gla_w_gate2, gla_b_gate, gla_norm_g, gla_w_out, ffn_w_up, ffn_w_down):
    n_b, seq, _ = x_prompt.shape
    n_seq, t_new, _ = x_sample.shape
    depth = norm_g.shape[0]
    n_valid = N_META_TOK + seq
    length = -(-n_valid // LANES) * LANES
    tm_p = length // 4
    tq = length // 8
    n_s_rows = n_seq * t_new
    dt = x_prompt.dtype

    xp = jnp.concatenate([jnp.broadcast_to(meta.astype(dt)[None], (n_b, N_META_TOK, D_MODEL)), x_prompt,
                          jnp.zeros((n_b, length - n_valid, D_MODEL), dt)], axis=1)
    xs = x_sample.reshape(1, n_s_rows, D_MODEL)
    tpad = 8

    def pad_rows(a):
        return jnp.pad(a.reshape(n_seq, t_new, a.shape[-1]), ((0, 0), (0, tpad - t_new), (0, 0)))

    n_pool = cache_k.shape[1]
    cache_k4 = cache_k.reshape(cache_k.shape[0], n_pool, PAGE, D_MODEL)
    cache_v4 = cache_v.reshape(cache_v.shape[0], n_pool, PAGE, D_MODEL)
    cache_lft = jnp.swapaxes(cache_logf, 2, 3)

    nk_p, nv_p, nf_p, ns_p = [], [], [], []
    nk_s, nv_s, nf_s, ns_s = [], [], [], []
    for i in range(depth):
        g = norm_g[i].reshape(4, 1, D_MODEL)
        j = i // 2
        if i % 2 == 0:
            w_in = fox_w_in[j]
            wq = w_in[:, :D_MODEL].astype(BF16)
            wk = w_in[:, D_MODEL:2 * D_MODEL].astype(BF16)
            wv = w_in[:, 2 * D_MODEL:3 * D_MODEL].astype(BF16)
            wf = jnp.pad(w_in[:, 3 * D_MODEL:], ((0, 0), (0, LANES - FOX_HEADS))).astype(BF16)
            bf = fox_b_f[j].reshape(1, FOX_HEADS)
            w_out = fox_w_out[j].astype(BF16)
            q16, k32, v32, k16, v16, lf = _fox_proj(xp, g[0], wq, wk, wv, wf, bf, tm_p)
            ccol, crow = _fox_cumsum(lf)
            o16 = _fox_attn(q16, k16, v16, ccol, crow, tq)
            xp = _fox_out(xp, o16, w_out, g[1], tm_p, n_valid)
            nk_p.append(k32[:, :n_valid].reshape(n_b, n_valid, FOX_HEADS, FOX_HEAD_DIM))
            nv_p.append(v32[:, :n_valid].reshape(n_b, n_valid, FOX_HEADS, FOX_HEAD_DIM))
            nf_p.append(lf[:, :n_valid])
            q16s, k32s, v32s, _, _, lfs = _fox_proj(xs, g[0], wq, wk, wv, wf, bf, n_s_rows)
            nt = _fox_new_bias(lfs[0], t_new)
            nt = jnp.transpose(nt.reshape(FOX_HEADS, n_seq, t_new), (1, 0, 2))
            nt = jnp.pad(nt, ((0, 0), (0, 0), (0, tpad - t_new)))
            os_ = _fox_sample(j, t_new, page_table, pad_rows(q16s.astype(F32)), pad_rows(k32s), pad_rows(v32s), nt,
                              cache_k4, cache_v4, cache_lft)
            xs = _fox_out(xs, os_[:, :t_new].reshape(1, n_s_rows, D_MODEL), w_out, g[1], n_s_rows, n_s_rows)
            nk_s.append(k32s.reshape(n_seq, t_new, FOX_HEADS, FOX_HEAD_DIM))
            nv_s.append(v32s.reshape(n_seq, t_new, FOX_HEADS, FOX_HEAD_DIM))
            nf_s.append(lfs.reshape(n_seq, t_new, FOX_HEADS))
        else:
            w_in = gla_w_in[j]
            o1, o2, o3, o4 = GLA_DK, 2 * GLA_DK, 2 * GLA_DK + GLA_DV, 2 * GLA_DK + 2 * GLA_DV
            wq = w_in[:, :o1].astype(BF16)
            wk = w_in[:, o1:o2].astype(BF16)
            wv = w_in[:, o2:o3].astype(BF16)
            wr = w_in[:, o3:o4].astype(BF16)
            wl = jnp.pad(w_in[:, o4:], ((0, 0), (0, LANES - GLA_RANK))).astype(BF16)
            wg2 = jnp.pad(gla_w_gate2[j], ((0, LANES - GLA_RANK), (0, 0))).astype(BF16)
            bg = gla_b_gate[j].reshape(1, GLA_DK)
            gn = gla_norm_g[j].reshape(1, GLA_DV)
            w_out = gla_w_out[j].astype(BF16)
            q, k, v, r, gate = _gla_proj(xp, g[0], wq, wk, wv, wr, wl, wg2, bg, tm_p)
            o, s_fin = _gla_prompt(q, k, gate, v, n_valid)
            xp = _gla_out(xp, o, r, gn, w_out, g[1], tm_p, n_valid)
            ns_p.append(s_fin)
            q, k, v, r, gate = _gla_proj(xs, g[0], wq, wk, wv, wr, wl, wg2, bg, n_s_rows)
            o, s_new = _gla_sample(j, t_new, pad_rows(q), pad_rows(k), pad_rows(gate), pad_rows(v.astype(F32)),
                                   state_gla, 4)
            xs = _gla_out(xs, o[:, :t_new].reshape(1, n_s_rows, GLA_DV), r, gn, w_out, g[1], n_s_rows, n_s_rows)
            ns_s.append(s_new)
        wup = jnp.transpose(ffn_w_up[i].astype(BF16).reshape(D_MODEL, 2 * N_FF_CHUNKS, FF_CHUNK), (1, 0, 2))
        wdn = ffn_w_down[i].astype(BF16).reshape(N_FF_CHUNKS, FF_CHUNK, D_MODEL)
        xp = _ffn(xp, g[2], g[3], wup, wdn, tm_p)
        xs = _ffn(xs, g[2], g[3], wup, wdn, n_s_rows)

    y_prompt = xp[:, N_META_TOK:n_valid]
    y_sample = xs.reshape(n_seq, t_new, D_MODEL)
    return (y_prompt, y_sample,
            jnp.stack(nk_p), jnp.stack(nv_p), jnp.stack(nf_p), jnp.stack(ns_p),
            jnp.stack(nk_s), jnp.stack(nv_s), jnp.stack(nf_s), jnp.stack(ns_s))
```

```python
import functools

import jax
import jax.numpy as jnp
from jax import lax
from jax.experimental import pallas as pl
from jax.experimental.pallas import tpu as pltpu

D_MODEL = 1024
N_META_TOK = 16
FOX_HEADS = 16
FOX_HEAD_DIM = 64
FOX_SCALE = FOX_HEAD_DIM ** -0.5
GLA_HEADS = 4
GLA_DK = 512
GLA_DV = 1024
GLA_DK_H = 128
GLA_DV_H = 256
GLA_SCALE = GLA_DK_H ** -0.5
GLA_RANK = 16
GLA_TAU = 16.0
D_FF = 2816
RMS_EPS = 1e-6
NEG = -1e30
PAGE = 128

LANES = 128
FF_CHUNK = 256
N_FF_CHUNKS = D_FF // FF_CHUNK
GLA_CHUNK = 128
GLA_SUB = 16
EXP_CLAMP = 80.0
VMEM_LIMIT = 56 * 1024 * 1024

F32 = jnp.float32
BF16 = jnp.bfloat16
NT = (((1,), (1,)), ((), ()))
TN = (((0,), (0,)), ((), ()))


def _cparams(sem):
    return pltpu.CompilerParams(dimension_semantics=sem, vmem_limit_bytes=VMEM_LIMIT)


def _rms(x, g):
    return x * lax.rsqrt(jnp.mean(x * x, axis=-1, keepdims=True) + RMS_EPS) * g


def _dot(a, b):
    return jnp.dot(a, b, preferred_element_type=F32)


def _select_dot(sel, x, dims=None, sel_first=True):
    hi = x.astype(BF16)
    r1 = x - hi.astype(F32)
    mid = r1.astype(BF16)
    lo = (r1 - mid.astype(F32)).astype(BF16)
    sel = sel.astype(BF16)
    out = None
    for piece in (hi, mid, lo):
        a, b = (sel, piece) if sel_first else (piece, sel)
        d = _dot(a, b) if dims is None else lax.dot_general(a, b, dims, preferred_element_type=F32)
        out = d if out is None else out + d
    return out


def _log_sigmoid(z):
    return jnp.minimum(z, 0.0) - jnp.log1p(jnp.exp(-jnp.abs(z)))


def _const_spec(shape):
    n = len(shape)
    return pl.BlockSpec(shape, lambda *_: (0,) * n, pipeline_mode=pl.Buffered(1))


def _row_spec(tm, width):
    return pl.BlockSpec((1, tm, width), lambda b, i: (b, i, 0))


def _fox_proj_kernel(x_ref, g_ref, wq_ref, wk_ref, wv_ref, wf_ref, bf_ref, k_all_ref, v_all_ref,
                     q16_ref, k_ref, v_ref, k16_ref, v16_ref, lf_ref):
    del k_all_ref, v_all_ref
    xn = _rms(x_ref[0], g_ref[...]).astype(BF16)
    q16_ref[0] = (_dot(xn, wq_ref[...]) * FOX_SCALE).astype(BF16)
    k = _dot(xn, wk_ref[...])
    k_ref[0] = k
    k16_ref[0] = k.astype(BF16)
    v = _dot(xn, wv_ref[...])
    v_ref[0] = v
    v16_ref[0] = v.astype(BF16)
    z = _dot(xn, wf_ref[...])[:, :FOX_HEADS] + bf_ref[...]
    lf_ref[0] = _log_sigmoid(z)


def _fox_proj(x, g, wq, wk, wv, wf, bf, tm, layer, k_all, v_all):
    nb, length, _ = x.shape
    sds = jax.ShapeDtypeStruct
    slab_spec = pl.BlockSpec((None, 1, tm, D_MODEL), lambda b, i: (layer, b, i, 0))
    slab_shape = sds(k_all.shape, F32)
    any_spec = pl.BlockSpec(memory_space=pl.ANY)
    n_in = 7
    return pl.pallas_call(
        _fox_proj_kernel,
        grid=(nb, length // tm),
        in_specs=[_row_spec(tm, D_MODEL), _const_spec((1, D_MODEL)),
                  _const_spec((D_MODEL, D_MODEL)), _const_spec((D_MODEL, D_MODEL)),
                  _const_spec((D_MODEL, D_MODEL)), _const_spec((D_MODEL, LANES)),
                  _const_spec((1, FOX_HEADS)), any_spec, any_spec],
        out_specs=[_row_spec(tm, D_MODEL), slab_spec, slab_spec, _row_spec(tm, D_MODEL), _row_spec(tm, D_MODEL),
                   _row_spec(tm, FOX_HEADS)],
        out_shape=[sds((nb, length, D_MODEL), BF16), slab_shape, slab_shape, sds((nb, length, D_MODEL), BF16),
                   sds((nb, length, D_MODEL), BF16), sds((nb, length, FOX_HEADS), F32)],
        input_output_aliases={n_in: 1, n_in + 1: 2},
        compiler_params=_cparams(("parallel", "parallel")),
        name="fox_proj",
    )(x, g, wq, wk, wv, wf, bf, k_all, v_all)


def _fox_cumsum_kernel(nblk, lf_ref, ccol_ref):
    r = lax.broadcasted_iota(jnp.int32, (LANES, LANES), 0)
    c = lax.broadcasted_iota(jnp.int32, (LANES, LANES), 1)
    tri = (c <= r).astype(F32)
    carry = jnp.zeros((1, FOX_HEADS), F32)
    for b in range(nblk):
        rows = slice(b * LANES, (b + 1) * LANES)
        cb = _select_dot(tri, lf_ref[0, rows, :]) + carry
        carry = cb[LANES - 1:LANES, :]
        ccol_ref[0, rows, :] = cb


def _fox_cumsum(lf):
    nb, length, _ = lf.shape
    return pl.pallas_call(
        functools.partial(_fox_cumsum_kernel, length // LANES),
        grid=(nb,),
        in_specs=[pl.BlockSpec((1, length, FOX_HEADS), lambda b: (b, 0, 0))],
        out_specs=pl.BlockSpec((1, length, FOX_HEADS), lambda b: (b, 0, 0)),
        out_shape=jax.ShapeDtypeStruct((nb, length, FOX_HEADS), F32),
        compiler_params=_cparams(("parallel",)),
        name="fox_cumsum",
    )(lf)


N_BIAS = 3
ATT_TQ = 256


def _fox_attn_kernel(q_ref, k_ref, v_ref, c_ref, o_ref, qa_ref, ka_ref, vt_ref, s_ref):
    hp = pl.program_id(1)
    length = q_ref.shape[1]
    half = FOX_HEAD_DIM

    c = c_ref[0]
    hi = c.astype(BF16)
    r1 = c - hi.astype(F32)
    mid = r1.astype(BF16)
    lo = (r1 - mid.astype(F32)).astype(BF16)
    hrow = lax.broadcasted_iota(jnp.int32, (FOX_HEADS, 2 * LANES), 0)
    col = lax.broadcasted_iota(jnp.int32, (FOX_HEADS, 2 * LANES), 1)
    is_k = col >= LANES
    lane_in = col - jnp.where(is_k, LANES, 0)
    start = jnp.where(hrow == 2 * hp, half, jnp.where(hrow == 2 * hp + 1, 0, -LANES)) + jnp.where(is_k, N_BIAS, 0)
    coef = jnp.where(is_k, -1.0, 1.0)
    ext = None
    for j, piece in enumerate((hi, mid, lo)):
        sel = jnp.where(lane_in == start + j, coef, 0.0).astype(BF16)
        d = _dot(piece, sel)
        ext = d if ext is None else ext + d
    lane1 = lax.broadcasted_iota(jnp.int32, (1, LANES), 1)
    within = lane1 % half
    ones_q = ((within >= N_BIAS) & (within < 2 * N_BIAS)).astype(F32)
    ones_k = (within < N_BIAS).astype(F32)
    q_ext = (ext[:, :LANES] + ones_q).astype(BF16)
    k_ext = (ext[:, LANES:] + ones_k).astype(BF16)
    first = lane1 < half
    q2 = q_ref[0]
    k2 = k_ref[0]
    qa_ref[0] = jnp.where(first, q2, q_ext)
    qa_ref[1] = jnp.where(first, q_ext, q2)
    ka_ref[0] = jnp.where(first, k2, k_ext)
    ka_ref[1] = jnp.where(first, k_ext, k2)
    for b in range(length // LANES):
        rows = slice(b * LANES, (b + 1) * LANES)
        vt_ref[:, rows] = v_ref[0, rows, :].astype(F32).T.astype(BF16)

    def block(r0, tq, n_chunks):
        kk = lax.broadcasted_iota(jnp.int32, (tq, tq), 0)
        qq = lax.broadcasted_iota(jnp.int32, (tq, tq), 1)
        visible = kk <= qq
        outs = []
        for i in range(2):
            qa = qa_ref[i, pl.ds(r0, tq), :]
            vt_rows = slice(i * half, (i + 1) * half)

            def scores(j, m):
                koff = pl.multiple_of(j * ATT_TQ, ATT_TQ)
                s = lax.dot_general(ka_ref[i, pl.ds(koff, ATT_TQ), :], qa, NT, preferred_element_type=F32)
                s_ref[pl.ds(koff, ATT_TQ), 0:tq] = s
                return jnp.maximum(m, jnp.max(s, axis=0, keepdims=True))

            sd = lax.dot_general(ka_ref[i, pl.ds(r0, tq), :], qa, NT, preferred_element_type=F32)
            sd = jnp.where(visible, sd, NEG)
            m = lax.fori_loop(0, n_chunks, scores, jnp.max(sd, axis=0, keepdims=True))
            pd = jnp.exp(sd - m)
            l0 = jnp.sum(pd, axis=0, keepdims=True)
            acc0 = _dot(vt_ref[vt_rows, pl.ds(r0, tq)], pd.astype(BF16))

            def weighted(j, carry):
                l, acc = carry
                koff = pl.multiple_of(j * ATT_TQ, ATT_TQ)
                p = jnp.exp(s_ref[pl.ds(koff, ATT_TQ), 0:tq] - m)
                l = l + jnp.sum(p, axis=0, keepdims=True)
                acc = acc + _dot(vt_ref[vt_rows, pl.ds(koff, ATT_TQ)], p.astype(BF16))
                return l, acc

            l, acc = lax.fori_loop(0, n_chunks, weighted, (l0, acc0))
            outs.append(acc / l)
        o_ref[0, pl.ds(r0, tq), :] = jnp.concatenate(outs, axis=0).T.astype(BF16)

    n_blocks = length // ATT_TQ

    def body(qb, _):
        block(pl.multiple_of(qb * ATT_TQ, ATT_TQ), ATT_TQ, qb)
        return 0

    lax.fori_loop(0, n_blocks, body, 0)
    tail = length - n_blocks * ATT_TQ
    if tail:
        block(n_blocks * ATT_TQ, tail, n_blocks)


def _fox_attn(q16, k16, v16, ccol):
    nb, length, _ = q16.shape
    n_hp = FOX_HEADS // 2
    col_spec = pl.BlockSpec((1, length, LANES), lambda b, h: (b, 0, h))
    return pl.pallas_call(
        _fox_attn_kernel,
        grid=(nb, n_hp),
        in_specs=[col_spec, col_spec, col_spec,
                  pl.BlockSpec((1, length, FOX_HEADS), lambda b, h: (b, 0, 0))],
        out_specs=col_spec,
        out_shape=jax.ShapeDtypeStruct((nb, length, D_MODEL), BF16),
        scratch_shapes=[pltpu.VMEM((2, length, LANES), BF16), pltpu.VMEM((2, length, LANES), BF16),
                        pltpu.VMEM((LANES, length), BF16),
                        pltpu.VMEM((length // ATT_TQ * ATT_TQ, ATT_TQ), F32)],
        compiler_params=_cparams(("parallel", "parallel")),
        name="fox_attn_prompt",
    )(q16, k16, v16, ccol)


def _fox_new_bias_kernel(t_new, lf_ref, nt_ref):
    n = lf_ref.shape[0]
    r = lax.broadcasted_iota(jnp.int32, (n, n), 0)
    c = lax.broadcasted_iota(jnp.int32, (n, n), 1)
    same_seq = (r // t_new) == (c // t_new)
    bd = (same_seq & (c <= r)).astype(F32)
    cum = _select_dot(bd, lf_ref[...])
    er = lax.broadcasted_iota(jnp.int32, (FOX_HEADS, FOX_HEADS), 0)
    ec = lax.broadcasted_iota(jnp.int32, (FOX_HEADS, FOX_HEADS), 1)
    eye = (er == ec).astype(F32)
    nt_ref[...] = _select_dot(eye, cum, NT)


def _fox_new_bias(lf, t_new):
    n = lf.shape[0]
    return pl.pallas_call(
        functools.partial(_fox_new_bias_kernel, t_new),
        out_shape=jax.ShapeDtypeStruct((FOX_HEADS, n), F32),
        compiler_params=pltpu.CompilerParams(vmem_limit_bytes=VMEM_LIMIT),
        name="fox_new_bias",
    )(lf)


def _fox_sample_kernel(layer, t_new, n_pages, pt_ref, q_ref, kn_ref, vn_ref, nt_ref, ck_hbm, cv_hbm, clf_hbm,
                       o_ref, kbuf, vbuf, lfbuf, sems):
    b = pl.program_id(0)
    n_seq = pl.num_programs(0)
    rows = t_new * FOX_HEADS

    def page_copies(seq, slot):
        copies = []
        for p in range(n_pages):
            page = pt_ref[seq * n_pages + p]
            copies.append(pltpu.make_async_copy(ck_hbm.at[layer, page], kbuf.at[slot, pl.ds(p * PAGE, PAGE)],
                                                sems.at[slot, 0]))
            copies.append(pltpu.make_async_copy(cv_hbm.at[layer, page], vbuf.at[slot, pl.ds(p * PAGE, PAGE)],
                                                sems.at[slot, 1]))
            copies.append(pltpu.make_async_copy(clf_hbm.at[layer, page],
                                                lfbuf.at[slot, pl.ds(p * FOX_HEADS, FOX_HEADS)], sems.at[slot, 2]))
        return copies

    slot = b % 2

    @pl.when(b == 0)
    def _():
        for cp in page_copies(0, 0):
            cp.start()

    @pl.when(b + 1 < n_seq)
    def _():
        for cp in page_copies(b + 1, 1 - slot):
            cp.start()

    row = lax.broadcasted_iota(jnp.int32, (rows, D_MODEL), 0)
    lane = lax.broadcasted_iota(jnp.int32, (rows, D_MODEL), 1)
    own_head = (row % FOX_HEADS) == (lane // FOX_HEAD_DIM)
    q = q_ref[0]
    qrep = jnp.concatenate([jnp.broadcast_to(q[t:t + 1, :], (FOX_HEADS, D_MODEL)) for t in range(t_new)], axis=0)
    qexp = jnp.where(own_head, qrep, 0.0).astype(BF16)
    qf = qexp.astype(F32)
    kn = kn_ref[0].astype(BF16).astype(F32)
    vn = vn_ref[0]
    nt = nt_ref[0]
    row_t = lax.broadcasted_iota(jnp.int32, (rows, 1), 0) // FOX_HEADS
    cols = []
    for t2 in range(t_new):
        sc = jnp.sum(qf * kn[t2:t2 + 1, :], axis=-1, keepdims=True)
        sc = sc - jnp.concatenate([nt[:, t2:t2 + 1]] * t_new, axis=0)
        cols.append(jnp.where(row_t >= t2, sc, NEG))
    m_new = functools.reduce(jnp.maximum, cols)

    for cp in page_copies(b, slot):
        cp.wait()

    lft = lfbuf[slot]
    j = lax.broadcasted_iota(jnp.int32, (PAGE, PAGE), 0)
    key = lax.broadcasted_iota(jnp.int32, (PAGE, PAGE), 1)
    within = _select_dot((j > key).astype(F32), lft, sel_first=False)
    total = jnp.broadcast_to(within[:, 0:1] + lft[:, 0:1], (n_pages * FOX_HEADS, PAGE))
    pr = lax.broadcasted_iota(jnp.int32, (n_pages * FOX_HEADS, n_pages * FOX_HEADS), 0)
    pc = lax.broadcasted_iota(jnp.int32, (n_pages * FOX_HEADS, n_pages * FOX_HEADS), 1)
    later_page = ((pr % FOX_HEADS) == (pc % FOX_HEADS)) & (pc // FOX_HEADS > pr // FOX_HEADS)
    suf = within + _select_dot(later_page.astype(F32), total)
    bias = jnp.concatenate(
        [jnp.concatenate([suf[p * FOX_HEADS:(p + 1) * FOX_HEADS, :]] * t_new, axis=0) for p in range(n_pages)], axis=1)

    st = lax.dot_general(kbuf[slot], qexp, NT, preferred_element_type=F32)
    s = st.T + bias
    m = jnp.maximum(m_new, jnp.max(s, axis=-1, keepdims=True))
    p_past = jnp.exp(s - m)
    p_new = [jnp.exp(sc - m) for sc in cols]
    l = jnp.sum(p_past, axis=-1, keepdims=True) + functools.reduce(jnp.add, p_new)
    acc = _dot(p_past.astype(BF16), vbuf[slot])
    acc = acc + functools.reduce(jnp.add, [p_new[t2] * vn[t2:t2 + 1, :] for t2 in range(t_new)])
    o = jnp.where(own_head, acc / l, 0.0)
    o_ref[0, 0:t_new, :] = o.reshape(t_new, FOX_HEADS, D_MODEL).sum(axis=1)
    o_ref[0, t_new:, :] = jnp.zeros((o_ref.shape[1] - t_new, D_MODEL), F32)


def _fox_sample(layer, t_new, page_table, q_new, k_new, v_new, nt, cache_k16, cache_v16, cache_lft):
    n_seq, tpad, _ = q_new.shape
    n_pages = page_table.shape[1]
    any_spec = pl.BlockSpec(memory_space=pl.ANY)
    grid_spec = pltpu.PrefetchScalarGridSpec(
        num_scalar_prefetch=1,
        grid=(n_seq,),
        in_specs=[pl.BlockSpec((1, tpad, D_MODEL), lambda b, pt: (b, 0, 0)),
                  pl.BlockSpec((1, tpad, D_MODEL), lambda b, pt: (b, 0, 0)),
                  pl.BlockSpec((1, tpad, D_MODEL), lambda b, pt: (b, 0, 0)),
                  pl.BlockSpec((1, FOX_HEADS, tpad), lambda b, pt: (b, 0, 0)),
                  any_spec, any_spec, any_spec],
        out_specs=pl.BlockSpec((1, tpad, D_MODEL), lambda b, pt: (b, 0, 0)),
        scratch_shapes=[pltpu.VMEM((2, n_pages * PAGE, D_MODEL), BF16), pltpu.VMEM((2, n_pages * PAGE, D_MODEL), BF16),
                        pltpu.VMEM((2, n_pages * FOX_HEADS, PAGE), F32), pltpu.SemaphoreType.DMA((2, 3))])
    return pl.pallas_call(
        functools.partial(_fox_sample_kernel, layer, t_new, n_pages),
        grid_spec=grid_spec,
        out_shape=jax.ShapeDtypeStruct((n_seq, tpad, D_MODEL), F32),
        compiler_params=_cparams(("arbitrary",)),
        name="fox_attn_sample",
    )(page_table.reshape(-1), q_new, k_new, v_new, nt, cache_k16, cache_v16, cache_lft)


def _valid_rows(tm, n_valid):
    row = pl.program_id(1) * tm + lax.broadcasted_iota(jnp.int32, (tm, 1), 0)
    return row < n_valid


def _fox_out_kernel(tm, n_valid, x_ref, o_ref, w_ref, g_ref, y_ref):
    m = _dot(o_ref[0].astype(BF16), w_ref[...])
    y_ref[0] = x_ref[0] + jnp.where(_valid_rows(tm, n_valid), _rms(m, g_ref[...]), 0.0)


def _fox_out(x, o, w, g, tm, n_valid):
    nb, length, _ = x.shape
    return pl.pallas_call(
        functools.partial(_fox_out_kernel, tm, n_valid),
        grid=(nb, length // tm),
        in_specs=[_row_spec(tm, D_MODEL), _row_spec(tm, D_MODEL),
                  _const_spec((D_MODEL, D_MODEL)), _const_spec((1, D_MODEL))],
        out_specs=_row_spec(tm, D_MODEL),
        out_shape=jax.ShapeDtypeStruct(x.shape, F32),
        compiler_params=_cparams(("parallel", "parallel")),
        name="fox_out",
    )(x, o, w, g)


def _gla_out_kernel(tm, n_valid, x_ref, o_ref, r_ref, gn_ref, w_ref, g_ref, y_ref):
    o = o_ref[0]
    r = r_ref[0]
    gn = gn_ref[...]
    parts = []
    for h in range(GLA_HEADS):
        cols = slice(h * GLA_DV_H, (h + 1) * GLA_DV_H)
        on = _rms(o[:, cols], gn[:, cols])
        rh = r[:, cols]
        parts.append((on * (rh * jax.nn.sigmoid(rh))).astype(BF16))
    m = _dot(jnp.concatenate(parts, axis=1), w_ref[...])
    y_ref[0] = x_ref[0] + jnp.where(_valid_rows(tm, n_valid), _rms(m, g_ref[...]), 0.0)


def _gla_out(x, o, r, gn, w, g, tm, n_valid):
    nb, length, _ = x.shape
    return pl.pallas_call(
        functools.partial(_gla_out_kernel, tm, n_valid),
        grid=(nb, length // tm),
        in_specs=[_row_spec(tm, D_MODEL), _row_spec(tm, GLA_DV), _row_spec(tm, GLA_DV),
                  _const_spec((1, GLA_DV)), _const_spec((GLA_DV, D_MODEL)), _const_spec((1, D_MODEL))],
        out_specs=_row_spec(tm, D_MODEL),
        out_shape=jax.ShapeDtypeStruct(x.shape, F32),
        compiler_params=_cparams(("parallel", "parallel")),
        name="gla_out",
    )(x, o, r, gn, w, g)


def _ffn_kernel(x_ref, g2_ref, g3_ref, wup_ref, wdn_ref, y_ref, xn_ref, acc_ref):
    x = x_ref[0]
    xn_ref[...] = _rms(x, g2_ref[...]).astype(BF16)
    acc_ref[...] = jnp.zeros(acc_ref.shape, F32)

    def chunk(c, _):
        xn = xn_ref[...]
        h1 = _dot(xn, wup_ref[c])
        h2 = _dot(xn, wup_ref[c + N_FF_CHUNKS])
        a = (h1 * jax.nn.sigmoid(h1) * h2).astype(BF16)
        acc_ref[...] += _dot(a, wdn_ref[c])
        return 0

    lax.fori_loop(0, N_FF_CHUNKS, chunk, 0)
    y_ref[0] = x + _rms(acc_ref[...], g3_ref[...])


def _ffn(x, g2, g3, wup, wdn, tm):
    nb, length, _ = x.shape
    return pl.pallas_call(
        _ffn_kernel,
        grid=(nb, length // tm),
        in_specs=[_row_spec(tm, D_MODEL), _const_spec((1, D_MODEL)), _const_spec((1, D_MODEL)),
                  _const_spec((2 * N_FF_CHUNKS, D_MODEL, FF_CHUNK)), _const_spec((N_FF_CHUNKS, FF_CHUNK, D_MODEL))],
        out_specs=_row_spec(tm, D_MODEL),
        out_shape=jax.ShapeDtypeStruct(x.shape, F32),
        scratch_shapes=[pltpu.VMEM((tm, D_MODEL), BF16), pltpu.VMEM((tm, D_MODEL), F32)],
        compiler_params=_cparams(("parallel", "parallel")),
        name="ffn",
    )(x, g2, g3, wup, wdn)


def _gla_proj_kernel(x_ref, g_ref, wq_ref, wk_ref, wv_ref, wr_ref, wl_ref, wg2_ref, bg_ref,
                     q_ref, k_ref, v_ref, r_ref, gate_ref):
    xn = _rms(x_ref[0], g_ref[...]).astype(BF16)
    q_ref[0] = _dot(xn, wq_ref[...]) * GLA_SCALE
    k_ref[0] = _dot(xn, wk_ref[...])
    v_ref[0] = _dot(xn, wv_ref[...]).astype(BF16)
    r_ref[0] = _dot(xn, wr_ref[...])
    low = _dot(xn, wl_ref[...]).astype(BF16)
    z = _dot(low, wg2_ref[...]) + bg_ref[...]
    gate_ref[0] = _log_sigmoid(z) * (1.0 / GLA_TAU)


def _gla_proj(x, g, wq, wk, wv, wr, wl, wg2, bg, tm):
    nb, length, _ = x.shape
    sds = jax.ShapeDtypeStruct
    return pl.pallas_call(
        _gla_proj_kernel,
        grid=(nb, length // tm),
        in_specs=[_row_spec(tm, D_MODEL), _const_spec((1, D_MODEL)),
                  _const_spec((D_MODEL, GLA_DK)), _const_spec((D_MODEL, GLA_DK)),
                  _const_spec((D_MODEL, GLA_DV)), _const_spec((D_MODEL, GLA_DV)),
                  _const_spec((D_MODEL, LANES)), _const_spec((LANES, GLA_DK)), _const_spec((1, GLA_DK))],
        out_specs=[_row_spec(tm, GLA_DK), _row_spec(tm, GLA_DK), _row_spec(tm, GLA_DV),
                   _row_spec(tm, GLA_DV), _row_spec(tm, GLA_DK)],
        out_shape=[sds((nb, length, GLA_DK), F32), sds((nb, length, GLA_DK), F32),
                   sds((nb, length, GLA_DV), BF16), sds((nb, length, GLA_DV), F32),
                   sds((nb, length, GLA_DK), F32)],
        compiler_params=_cparams(("parallel", "parallel")),
        name="gla_proj",
    )(x, g, wq, wk, wv, wr, wl, wg2, bg)


def _gla_prompt_kernel(n_chunks, n_valid, q_ref, k_ref, g_ref, v_ref, o_ref, s_out_ref, s_ref):
    s_ref[...] = jnp.zeros(s_ref.shape, F32)
    r = lax.broadcasted_iota(jnp.int32, (GLA_CHUNK, GLA_CHUNK), 0)
    c = lax.broadcasted_iota(jnp.int32, (GLA_CHUNK, GLA_CHUNK), 1)
    causal = c <= r
    tri = causal.astype(F32)
    n_sub = GLA_CHUNK // GLA_SUB

    def chunk(ci, _):
        off = pl.multiple_of(ci * GLA_CHUNK, GLA_CHUNK)
        rows = pl.ds(off, GLA_CHUNK)
        pos = off + lax.broadcasted_iota(jnp.int32, (GLA_CHUNK, 1), 0)
        g = jnp.where(pos < n_valid, g_ref[0, rows, :], 0.0)
        q = q_ref[0, rows, :]
        k = k_ref[0, rows, :]
        v = v_ref[0, rows, :]
        big_g = _select_dot(tri, g)
        g_last = big_g[GLA_CHUNK - 1:GLA_CHUNK, :]
        g_last_col = big_g.T[:, GLA_CHUNK - 1:GLA_CHUNK]
        state = s_ref[...]
        o = _dot((q * jnp.exp(big_g)).astype(BF16), state.astype(BF16))
        a_parts = []
        for i in range(n_sub):
            lo, hi = i * GLA_SUB, (i + 1) * GLA_SUB
            ref = big_g[lo - 1:lo, :] if i > 0 else jnp.zeros((1, GLA_DK_H), F32)
            q_i = (q[lo:hi, :] * jnp.exp(big_g[lo:hi, :] - ref)).astype(BF16)
            e = jnp.where(r < hi, jnp.exp(jnp.minimum(ref - big_g, EXP_CLAMP)), 0.0)
            a_parts.append(lax.dot_general(q_i, (k * e).astype(BF16), NT, preferred_element_type=F32))
        a = jnp.where(causal, jnp.concatenate(a_parts, axis=0), 0.0)
        o_ref[0, rows, :] = o + _dot(a.astype(BF16), v)
        k_dec = (k * jnp.exp(g_last - big_g)).T.astype(BF16)
        s_ref[...] = state * jnp.exp(g_last_col) + _dot(k_dec, v)
        return 0

    lax.fori_loop(0, n_chunks, chunk, 0)
    s_out_ref[0, 0] = s_ref[...]


def _gla_prompt(q, k, g, v, n_valid):
    nb, length, _ = q.shape
    sds = jax.ShapeDtypeStruct
    kspec = pl.BlockSpec((1, length, GLA_DK_H), lambda b, h: (b, 0, h))
    vspec = pl.BlockSpec((1, length, GLA_DV_H), lambda b, h: (b, 0, h))
    return pl.pallas_call(
        functools.partial(_gla_prompt_kernel, length // GLA_CHUNK, n_valid),
        grid=(nb, GLA_HEADS),
        in_specs=[kspec, kspec, kspec, vspec],
        out_specs=[vspec, pl.BlockSpec((1, 1, GLA_DK_H, GLA_DV_H), lambda b, h: (b, h, 0, 0))],
        out_shape=[sds((nb, length, GLA_DV), F32), sds((nb, GLA_HEADS, GLA_DK_H, GLA_DV_H), F32)],
        scratch_shapes=[pltpu.VMEM((GLA_DK_H, GLA_DV_H), F32)],
        compiler_params=_cparams(("parallel", "parallel")),
        name="gla_prompt",
    )(q, k, g, v)


def _gla_sample_kernel(sb, t_new, q_ref, k_ref, g_ref, v_ref, s_in_ref, o_ref, s_out_ref):
    tpad = q_ref.shape[1]
    rowi = lax.broadcasted_iota(jnp.int32, (tpad, GLA_DK_H), 0)
    rowc = lax.broadcasted_iota(jnp.int32, (tpad, 1), 0)
    for i in range(sb):
        for h in range(GLA_HEADS):
            kcols = slice(h * GLA_DK_H, (h + 1) * GLA_DK_H)
            vcols = slice(h * GLA_DV_H, (h + 1) * GLA_DV_H)
            g = g_ref[i, :, kcols]
            q = q_ref[i, :, kcols]
            k = k_ref[i, :, kcols]
            v = v_ref[i, :, vcols]
            big_g = jnp.zeros((tpad, GLA_DK_H), F32)
            for u in range(t_new):
                big_g = big_g + jnp.where(rowi >= u, g[u:u + 1, :], 0.0)
            state = s_in_ref[0, i, h]
            o = _dot(q * jnp.exp(big_g), state)
            for s in range(t_new):
                w = jnp.exp(jnp.where(rowi >= s, big_g - big_g[s:s + 1, :], 0.0))
                a = jnp.sum(q * k[s:s + 1, :] * w, axis=-1, keepdims=True)
                o = o + jnp.where(rowc >= s, a, 0.0) * v[s:s + 1, :]
            o_ref[i, :, vcols] = o
            g_last = big_g[t_new - 1:t_new, :]
            k_dec = k * jnp.exp(g_last - big_g)
            g_last_col = jnp.broadcast_to(g_last, (tpad, GLA_DK_H)).T[:, 0:1]
            s_out_ref[0, i, h] = state * jnp.exp(g_last_col) + lax.dot_general(k_dec, v, TN, preferred_element_type=F32)


def _gla_sample(layer, t_new, q, k, g, v, state_all, sb):
    n_seq, tpad, _ = q.shape
    sds = jax.ShapeDtypeStruct
    kspec = pl.BlockSpec((sb, tpad, GLA_DK), lambda b: (b, 0, 0))
    vspec = pl.BlockSpec((sb, tpad, GLA_DV), lambda b: (b, 0, 0))
    sspec_in = pl.BlockSpec((1, sb, GLA_HEADS, GLA_DK_H, GLA_DV_H), lambda b: (layer, b, 0, 0, 0))
    sspec_out = pl.BlockSpec((1, sb, GLA_HEADS, GLA_DK_H, GLA_DV_H), lambda b: (0, b, 0, 0, 0))
    o, s_new = pl.pallas_call(
        functools.partial(_gla_sample_kernel, sb, t_new),
        grid=(n_seq // sb,),
        in_specs=[kspec, kspec, kspec, vspec, sspec_in],
        out_specs=[vspec, sspec_out],
        out_shape=[sds((n_seq, tpad, GLA_DV), F32), sds((1, n_seq, GLA_HEADS, GLA_DK_H, GLA_DV_H), F32)],
        compiler_params=_cparams(("parallel",)),
        name="gla_sample",
    )(q, k, g, v, state_all)
    return o, s_new[0]


def kernel(x_prompt, x_sample, cache_k, cache_v, cache_logf, state_gla, page_table, meta, norm_g, fox_w_in, fox_b_f, fox_w_out, gla_w_in, gla_w_gate2, gla_b_gate, gla_norm_g, gla_w_out, ffn_w_up, ffn_w_down):
    n_b, seq, _ = x_prompt.shape
    n_seq, t_new, _ = x_sample.shape
    depth = norm_g.shape[0]
    n_valid = N_META_TOK + seq
    length = -(-n_valid // LANES) * LANES
    tm_p = length // 4
    n_s_rows = n_seq * t_new
    dt = x_prompt.dtype

    xp = jnp.concatenate([jnp.broadcast_to(meta.astype(dt)[None], (n_b, N_META_TOK, D_MODEL)), x_prompt,
                          jnp.zeros((n_b, length - n_valid, D_MODEL), dt)], axis=1)
    xs = x_sample.reshape(1, n_s_rows, D_MODEL)
    tpad = 8

    def pad_rows(a):
        return jnp.pad(a.reshape(n_seq, t_new, a.shape[-1]), ((0, 0), (0, tpad - t_new), (0, 0)))

    n_pool = cache_k.shape[1]
    cache_k16 = cache_k.reshape(cache_k.shape[0], n_pool, PAGE, D_MODEL).astype(BF16)
    cache_v16 = cache_v.reshape(cache_v.shape[0], n_pool, PAGE, D_MODEL).astype(BF16)
    cache_lft = jnp.swapaxes(cache_logf, 2, 3)

    n_fox = (depth + 1) // 2
    kv_p = [jnp.zeros((n_fox, n_b, n_valid, D_MODEL), F32) for _ in range(2)]
    kv_s = [jnp.zeros((n_fox, 1, n_s_rows, D_MODEL), F32) for _ in range(2)]
    nf_p, ns_p, nf_s, ns_s = [], [], [], []
    for i in range(depth):
        g = norm_g[i].reshape(4, 1, D_MODEL)
        j = i // 2
        if i % 2 == 0:
            w_in = fox_w_in[j]
            wq = w_in[:, :D_MODEL].astype(BF16)
            wk = w_in[:, D_MODEL:2 * D_MODEL].astype(BF16)
            wv = w_in[:, 2 * D_MODEL:3 * D_MODEL].astype(BF16)
            wf = jnp.pad(w_in[:, 3 * D_MODEL:], ((0, 0), (0, LANES - FOX_HEADS))).astype(BF16)
            bf = fox_b_f[j].reshape(1, FOX_HEADS)
            w_out = fox_w_out[j].astype(BF16)
            q16, *kv_p, k16, v16, lf = _fox_proj(xp, g[0], wq, wk, wv, wf, bf, tm_p, j, *kv_p)
            ccol = _fox_cumsum(lf)
            o16 = _fox_attn(q16, k16, v16, ccol)
            xp = _fox_out(xp, o16, w_out, g[1], tm_p, n_valid)
            nf_p.append(lf[:, :n_valid])
            q16s, *kv_s, _, _, lfs = _fox_proj(xs, g[0], wq, wk, wv, wf, bf, n_s_rows, j, *kv_s)
            k32s, v32s = kv_s[0][j], kv_s[1][j]
            nt = _fox_new_bias(lfs[0], t_new)
            nt = jnp.transpose(nt.reshape(FOX_HEADS, n_seq, t_new), (1, 0, 2))
            nt = jnp.pad(nt, ((0, 0), (0, 0), (0, tpad - t_new)))
            os_ = _fox_sample(j, t_new, page_table, pad_rows(q16s.astype(F32)), pad_rows(k32s), pad_rows(v32s), nt,
                              cache_k16, cache_v16, cache_lft)
            xs = _fox_out(xs, os_[:, :t_new].reshape(1, n_s_rows, D_MODEL), w_out, g[1], n_s_rows, n_s_rows)
            nf_s.append(lfs.reshape(n_seq, t_new, FOX_HEADS))
        else:
            w_in = gla_w_in[j]
            o1, o2, o3, o4 = GLA_DK, 2 * GLA_DK, 2 * GLA_DK + GLA_DV, 2 * GLA_DK + 2 * GLA_DV
            wq = w_in[:, :o1].astype(BF16)
            wk = w_in[:, o1:o2].astype(BF16)
            wv = w_in[:, o2:o3].astype(BF16)
            wr = w_in[:, o3:o4].astype(BF16)
            wl = jnp.pad(w_in[:, o4:], ((0, 0), (0, LANES - GLA_RANK))).astype(BF16)
            wg2 = jnp.pad(gla_w_gate2[j], ((0, LANES - GLA_RANK), (0, 0))).astype(BF16)
            bg = gla_b_gate[j].reshape(1, GLA_DK)
            gn = gla_norm_g[j].reshape(1, GLA_DV)
            w_out = gla_w_out[j].astype(BF16)
            q, k, v, r, gate = _gla_proj(xp, g[0], wq, wk, wv, wr, wl, wg2, bg, tm_p)
            o, s_fin = _gla_prompt(q, k, gate, v, n_valid)
            xp = _gla_out(xp, o, r, gn, w_out, g[1], tm_p, n_valid)
            ns_p.append(s_fin)
            q, k, v, r, gate = _gla_proj(xs, g[0], wq, wk, wv, wr, wl, wg2, bg, n_s_rows)
            o, s_new = _gla_sample(j, t_new, pad_rows(q), pad_rows(k), pad_rows(gate), pad_rows(v.astype(F32)),
                                   state_gla, 4)
            xs = _gla_out(xs, o[:, :t_new].reshape(1, n_s_rows, GLA_DV), r, gn, w_out, g[1], n_s_rows, n_s_rows)
            ns_s.append(s_new)
        wup = jnp.transpose(ffn_w_up[i].astype(BF16).reshape(D_MODEL, 2 * N_FF_CHUNKS, FF_CHUNK), (1, 0, 2))
        wdn = ffn_w_down[i].astype(BF16).reshape(N_FF_CHUNKS, FF_CHUNK, D_MODEL)
        xp = _ffn(xp, g[2], g[3], wup, wdn, tm_p)
        xs = _ffn(xs, g[2], g[3], wup, wdn, n_s_rows)

    y_prompt = xp[:, N_META_TOK:n_valid]
    y_sample = xs.reshape(n_seq, t_new, D_MODEL)
    heads = lambda a, rows: a.reshape(n_fox, -1, rows, FOX_HEADS, FOX_HEAD_DIM)
    return (y_prompt, y_sample,
            heads(kv_p[0], n_valid), heads(kv_p[1], n_valid), jnp.stack(nf_p), jnp.stack(ns_p),
            heads(kv_s[0], t_new), heads(kv_s[1], t_new), jnp.stack(nf_s), jnp.stack(ns_s))
```

```python
import functools

import jax
import jax.numpy as jnp
from jax import lax
from jax.experimental import pallas as pl
from jax.experimental.pallas import tpu as pltpu

D_MODEL = 1024
N_META_TOK = 16
FOX_HEADS = 16
FOX_HEAD_DIM = 64
FOX_SCALE = FOX_HEAD_DIM ** -0.5
GLA_HEADS = 4
GLA_DK = 512
GLA_DV = 1024
GLA_DK_H = 128
GLA_DV_H = 256
GLA_SCALE = GLA_DK_H ** -0.5
GLA_RANK = 16
GLA_TAU = 16.0
D_FF = 2816
RMS_EPS = 1e-6
NEG = -1e30
PAGE = 128

LANES = 128
FF_CHUNK = 256
N_FF_CHUNKS = D_FF // FF_CHUNK
GLA_CHUNK = 128
GLA_SUB = 16
EXP_CLAMP = 80.0
VMEM_LIMIT = 56 * 1024 * 1024

F32 = jnp.float32
BF16 = jnp.bfloat16
NT = (((1,), (1,)), ((), ()))
TN = (((0,), (0,)), ((), ()))


def _cparams(sem):
    return pltpu.CompilerParams(dimension_semantics=sem, vmem_limit_bytes=VMEM_LIMIT)


def _rms(x, g):
    return x * lax.rsqrt(jnp.mean(x * x, axis=-1, keepdims=True) + RMS_EPS) * g


def _dot(a, b):
    return jnp.dot(a, b, preferred_element_type=F32)


def _select_dot(sel, x, dims=None, sel_first=True):
    hi = x.astype(BF16)
    r1 = x - hi.astype(F32)
    mid = r1.astype(BF16)
    lo = (r1 - mid.astype(F32)).astype(BF16)
    sel = sel.astype(BF16)
    out = None
    for piece in (hi, mid, lo):
        a, b = (sel, piece) if sel_first else (piece, sel)
        d = _dot(a, b) if dims is None else lax.dot_general(a, b, dims, preferred_element_type=F32)
        out = d if out is None else out + d
    return out


def _log_sigmoid(z):
    return jnp.minimum(z, 0.0) - jnp.log1p(jnp.exp(-jnp.abs(z)))


def _const_spec(shape):
    n = len(shape)
    return pl.BlockSpec(shape, lambda *_: (0,) * n, pipeline_mode=pl.Buffered(1))


def _row_spec(tm, width):
    return pl.BlockSpec((1, tm, width), lambda b, i: (b, i, 0))


def _fox_proj_kernel(x_ref, g_ref, wq_ref, wk_ref, wv_ref, wf_ref, bf_ref, k_all_ref, v_all_ref,
                     q16_ref, k_ref, v_ref, k16_ref, v16_ref, lf_ref):
    del k_all_ref, v_all_ref
    xn = _rms(x_ref[0], g_ref[...]).astype(BF16)
    q16_ref[0] = (_dot(xn, wq_ref[...]) * FOX_SCALE).astype(BF16)
    k = _dot(xn, wk_ref[...])
    k_ref[0] = k
    k16_ref[0] = k.astype(BF16)
    v = _dot(xn, wv_ref[...])
    v_ref[0] = v
    v16_ref[0] = v.astype(BF16)
    z = _dot(xn, wf_ref[...])[:, :FOX_HEADS] + bf_ref[...]
    lf_ref[0] = _log_sigmoid(z)


def _fox_proj(x, g, wq, wk, wv, wf, bf, tm, layer, k_all, v_all):
    nb, length, _ = x.shape
    sds = jax.ShapeDtypeStruct
    slab_spec = pl.BlockSpec((None, 1, tm, D_MODEL), lambda b, i: (layer, b, i, 0))
    slab_shape = sds(k_all.shape, F32)
    any_spec = pl.BlockSpec(memory_space=pl.ANY)
    n_in = 7
    return pl.pallas_call(
        _fox_proj_kernel,
        grid=(nb, length // tm),
        in_specs=[_row_spec(tm, D_MODEL), _const_spec((1, D_MODEL)),
                  _const_spec((D_MODEL, D_MODEL)), _const_spec((D_MODEL, D_MODEL)),
                  _const_spec((D_MODEL, D_MODEL)), _const_spec((D_MODEL, LANES)),
                  _const_spec((1, FOX_HEADS)), any_spec, any_spec],
        out_specs=[_row_spec(tm, D_MODEL), slab_spec, slab_spec, _row_spec(tm, D_MODEL), _row_spec(tm, D_MODEL),
                   _row_spec(tm, FOX_HEADS)],
        out_shape=[sds((nb, length, D_MODEL), BF16), slab_shape, slab_shape, sds((nb, length, D_MODEL), BF16),
                   sds((nb, length, D_MODEL), BF16), sds((nb, length, FOX_HEADS), F32)],
        input_output_aliases={n_in: 1, n_in + 1: 2},
        compiler_params=_cparams(("parallel", "parallel")),
        name="fox_proj",
    )(x, g, wq, wk, wv, wf, bf, k_all, v_all)


def _fox_cumsum_kernel(nblk, lf_ref, ccol_ref):
    r = lax.broadcasted_iota(jnp.int32, (LANES, LANES), 0)
    c = lax.broadcasted_iota(jnp.int32, (LANES, LANES), 1)
    tri = (c <= r).astype(F32)
    carry = jnp.zeros((1, FOX_HEADS), F32)
    for b in range(nblk):
        rows = slice(b * LANES, (b + 1) * LANES)
        cb = _select_dot(tri, lf_ref[0, rows, :]) + carry
        carry = cb[LANES - 1:LANES, :]
        ccol_ref[0, rows, :] = cb


def _fox_cumsum(lf):
    nb, length, _ = lf.shape
    return pl.pallas_call(
        functools.partial(_fox_cumsum_kernel, length // LANES),
        grid=(nb,),
        in_specs=[pl.BlockSpec((1, length, FOX_HEADS), lambda b: (b, 0, 0))],
        out_specs=pl.BlockSpec((1, length, FOX_HEADS), lambda b: (b, 0, 0)),
        out_shape=jax.ShapeDtypeStruct((nb, length, FOX_HEADS), F32),
        compiler_params=_cparams(("parallel",)),
        name="fox_cumsum",
    )(lf)


N_BIAS = 3
ATT_TQ = 256


def _fox_attn_kernel(q_ref, k_ref, v_ref, c_ref, o_ref, qa_ref, ka_ref, vt_ref, s_ref):
    hp = pl.program_id(1)
    length = q_ref.shape[1]
    half = FOX_HEAD_DIM

    c = c_ref[0]
    hi = c.astype(BF16)
    r1 = c - hi.astype(F32)
    mid = r1.astype(BF16)
    lo = (r1 - mid.astype(F32)).astype(BF16)
    hrow = lax.broadcasted_iota(jnp.int32, (FOX_HEADS, 2 * LANES), 0)
    col = lax.broadcasted_iota(jnp.int32, (FOX_HEADS, 2 * LANES), 1)
    is_k = col >= LANES
    lane_in = col - jnp.where(is_k, LANES, 0)
    start = jnp.where(hrow == 2 * hp, half, jnp.where(hrow == 2 * hp + 1, 0, -LANES)) + jnp.where(is_k, N_BIAS, 0)
    coef = jnp.where(is_k, -1.0, 1.0)
    ext = None
    for j, piece in enumerate((hi, mid, lo)):
        sel = jnp.where(lane_in == start + j, coef, 0.0).astype(BF16)
        d = _dot(piece, sel)
        ext = d if ext is None else ext + d
    lane1 = lax.broadcasted_iota(jnp.int32, (1, LANES), 1)
    within = lane1 % half
    ones_q = ((within >= N_BIAS) & (within < 2 * N_BIAS)).astype(F32)
    ones_k = (within < N_BIAS).astype(F32)
    q_ext = (ext[:, :LANES] + ones_q).astype(BF16)
    k_ext = (ext[:, LANES:] + ones_k).astype(BF16)
    first = lane1 < half
    q2 = q_ref[0]
    k2 = k_ref[0]
    qa_ref[0] = jnp.where(first, q2, q_ext)
    qa_ref[1] = jnp.where(first, q_ext, q2)
    ka_ref[0] = jnp.where(first, k2, k_ext)
    ka_ref[1] = jnp.where(first, k_ext, k2)
    for b in range(length // LANES):
        rows = slice(b * LANES, (b + 1) * LANES)
        vt_ref[:, rows] = v_ref[0, rows, :].astype(F32).T.astype(BF16)

    def block(r0, tq, n_chunks):
        kk = lax.broadcasted_iota(jnp.int32, (tq, tq), 0)
        qq = lax.broadcasted_iota(jnp.int32, (tq, tq), 1)
        visible = kk <= qq
        qa = [qa_ref[i, pl.ds(r0, tq), :] for i in range(2)]
        vt_rows = [slice(i * half, (i + 1) * half) for i in range(2)]

        def sweep(fn, carry):
            def pair(j2, c):
                return fn(2 * j2 + 1, fn(2 * j2, c))
            carry = lax.fori_loop(0, n_chunks // 2, pair, carry)
            return lax.fori_loop(2 * (n_chunks // 2), n_chunks, fn, carry)

        def scores(j, ms):
            koff = pl.multiple_of(j * ATT_TQ, ATT_TQ)
            out = []
            for i in range(2):
                s = lax.dot_general(ka_ref[i, pl.ds(koff, ATT_TQ), :], qa[i], NT, preferred_element_type=F32)
                s_ref[i, pl.ds(koff, ATT_TQ), 0:tq] = s
                out.append(jnp.maximum(ms[i], jnp.max(s, axis=0, keepdims=True)))
            return tuple(out)

        sd = [jnp.where(visible, lax.dot_general(ka_ref[i, pl.ds(r0, tq), :], qa[i], NT, preferred_element_type=F32),
                        NEG) for i in range(2)]
        ms = sweep(scores, tuple(jnp.max(sd[i], axis=0, keepdims=True) for i in range(2)))
        init = []
        for i in range(2):
            pd = jnp.exp(sd[i] - ms[i])
            init += [jnp.sum(pd, axis=0, keepdims=True), _dot(vt_ref[vt_rows[i], pl.ds(r0, tq)], pd.astype(BF16))]

        def weighted(j, carry):
            koff = pl.multiple_of(j * ATT_TQ, ATT_TQ)
            out = []
            for i in range(2):
                l, acc = carry[2 * i], carry[2 * i + 1]
                p = jnp.exp(s_ref[i, pl.ds(koff, ATT_TQ), 0:tq] - ms[i])
                out += [l + jnp.sum(p, axis=0, keepdims=True),
                        acc + _dot(vt_ref[vt_rows[i], pl.ds(koff, ATT_TQ)], p.astype(BF16))]
            return tuple(out)

        la, acca, lb, accb = sweep(weighted, tuple(init))
        o_ref[0, pl.ds(r0, tq), :] = jnp.concatenate([acca / la, accb / lb], axis=0).T.astype(BF16)

    n_blocks = length // ATT_TQ

    def body(qb, _):
        block(pl.multiple_of(qb * ATT_TQ, ATT_TQ), ATT_TQ, qb)
        return 0

    lax.fori_loop(0, n_blocks, body, 0)
    tail = length - n_blocks * ATT_TQ
    if tail:
        block(n_blocks * ATT_TQ, tail, n_blocks)


def _fox_attn(q16, k16, v16, ccol):
    nb, length, _ = q16.shape
    n_hp = FOX_HEADS // 2
    col_spec = pl.BlockSpec((1, length, LANES), lambda b, h: (b, 0, h))
    return pl.pallas_call(
        _fox_attn_kernel,
        grid=(nb, n_hp),
        in_specs=[col_spec, col_spec, col_spec,
                  pl.BlockSpec((1, length, FOX_HEADS), lambda b, h: (b, 0, 0))],
        out_specs=col_spec,
        out_shape=jax.ShapeDtypeStruct((nb, length, D_MODEL), BF16),
        scratch_shapes=[pltpu.VMEM((2, length, LANES), BF16), pltpu.VMEM((2, length, LANES), BF16),
                        pltpu.VMEM((LANES, length), BF16),
                        pltpu.VMEM((2, length // ATT_TQ * ATT_TQ, ATT_TQ), F32)],
        compiler_params=_cparams(("parallel", "parallel")),
        name="fox_attn_prompt",
    )(q16, k16, v16, ccol)


def _fox_new_bias_kernel(t_new, lf_ref, nt_ref):
    n = lf_ref.shape[0]
    r = lax.broadcasted_iota(jnp.int32, (n, n), 0)
    c = lax.broadcasted_iota(jnp.int32, (n, n), 1)
    same_seq = (r // t_new) == (c // t_new)
    bd = (same_seq & (c <= r)).astype(F32)
    cum = _select_dot(bd, lf_ref[...])
    er = lax.broadcasted_iota(jnp.int32, (FOX_HEADS, FOX_HEADS), 0)
    ec = lax.broadcasted_iota(jnp.int32, (FOX_HEADS, FOX_HEADS), 1)
    eye = (er == ec).astype(F32)
    nt_ref[...] = _select_dot(eye, cum, NT)


def _fox_new_bias(lf, t_new):
    n = lf.shape[0]
    return pl.pallas_call(
        functools.partial(_fox_new_bias_kernel, t_new),
        out_shape=jax.ShapeDtypeStruct((FOX_HEADS, n), F32),
        compiler_params=pltpu.CompilerParams(vmem_limit_bytes=VMEM_LIMIT),
        name="fox_new_bias",
    )(lf)


def _fox_sample_kernel(layer, t_new, n_pages, pt_ref, q_ref, kn_ref, vn_ref, nt_ref, ck_hbm, cv_hbm, clf_hbm,
                       o_ref, kbuf, vbuf, lfbuf, sems):
    b = pl.program_id(0)
    n_seq = pl.num_programs(0)
    rows = t_new * FOX_HEADS

    def page_copies(seq, slot):
        copies = []
        for p in range(n_pages):
            page = pt_ref[seq * n_pages + p]
            copies.append(pltpu.make_async_copy(ck_hbm.at[layer, page], kbuf.at[slot, :, pl.ds(p * PAGE, PAGE)],
                                                sems.at[slot, 0]))
            copies.append(pltpu.make_async_copy(cv_hbm.at[layer, page], vbuf.at[slot, :, pl.ds(p * PAGE, PAGE)],
                                                sems.at[slot, 1]))
            copies.append(pltpu.make_async_copy(clf_hbm.at[layer, page],
                                                lfbuf.at[slot, pl.ds(p * FOX_HEADS, FOX_HEADS)], sems.at[slot, 2]))
        return copies

    slot = b % 2

    @pl.when(b == 0)
    def _():
        for cp in page_copies(0, 0):
            cp.start()

    @pl.when(b + 1 < n_seq)
    def _():
        for cp in page_copies(b + 1, 1 - slot):
            cp.start()

    row = lax.broadcasted_iota(jnp.int32, (rows, D_MODEL), 0)
    lane = lax.broadcasted_iota(jnp.int32, (rows, D_MODEL), 1)
    own_head = (row % FOX_HEADS) == (lane // FOX_HEAD_DIM)
    q = q_ref[0]
    qrep = jnp.concatenate([jnp.broadcast_to(q[t:t + 1, :], (FOX_HEADS, D_MODEL)) for t in range(t_new)], axis=0)
    qexp = jnp.where(own_head, qrep, 0.0).astype(BF16)
    qf = qexp.astype(F32)
    kn = kn_ref[0].astype(BF16).astype(F32)
    vn = vn_ref[0]
    nt = nt_ref[0]
    row_t = lax.broadcasted_iota(jnp.int32, (rows, 1), 0) // FOX_HEADS
    cols = []
    for t2 in range(t_new):
        sc = jnp.sum(qf * kn[t2:t2 + 1, :], axis=-1, keepdims=True)
        sc = sc - jnp.concatenate([nt[:, t2:t2 + 1]] * t_new, axis=0)
        cols.append(jnp.where(row_t >= t2, sc, NEG))
    m_new = functools.reduce(jnp.maximum, cols)

    for cp in page_copies(b, slot):
        cp.wait()

    lft = lfbuf[slot]
    j = lax.broadcasted_iota(jnp.int32, (PAGE, PAGE), 0)
    key = lax.broadcasted_iota(jnp.int32, (PAGE, PAGE), 1)
    within = _select_dot((j > key).astype(F32), lft, sel_first=False)
    total = jnp.broadcast_to(within[:, 0:1] + lft[:, 0:1], (n_pages * FOX_HEADS, PAGE))
    pr = lax.broadcasted_iota(jnp.int32, (n_pages * FOX_HEADS, n_pages * FOX_HEADS), 0)
    pc = lax.broadcasted_iota(jnp.int32, (n_pages * FOX_HEADS, n_pages * FOX_HEADS), 1)
    later_page = ((pr % FOX_HEADS) == (pc % FOX_HEADS)) & (pc // FOX_HEADS > pr // FOX_HEADS)
    suf = within + _select_dot(later_page.astype(F32), total)
    bias = jnp.concatenate(
        [jnp.concatenate([suf[p * FOX_HEADS:(p + 1) * FOX_HEADS, :]] * t_new, axis=0) for p in range(n_pages)], axis=1)

    s = _dot(qf, kbuf[slot]) + bias
    m = jnp.maximum(m_new, jnp.max(s, axis=-1, keepdims=True))
    p_past = jnp.exp(s - m)
    p_new = [jnp.exp(sc - m) for sc in cols]
    l = jnp.sum(p_past, axis=-1, keepdims=True) + functools.reduce(jnp.add, p_new)
    acc = _dot(vbuf[slot], p_past.T).T
    acc = acc + functools.reduce(jnp.add, [p_new[t2] * vn[t2:t2 + 1, :] for t2 in range(t_new)])
    o = jnp.where(own_head, acc / l, 0.0)
    o_ref[0, 0:t_new, :] = o.reshape(t_new, FOX_HEADS, D_MODEL).sum(axis=1)
    o_ref[0, t_new:, :] = jnp.zeros((o_ref.shape[1] - t_new, D_MODEL), F32)


def _fox_sample(layer, t_new, page_table, q_new, k_new, v_new, nt, cache_kt, cache_vt, cache_lft):
    n_seq, tpad, _ = q_new.shape
    n_pages = page_table.shape[1]
    any_spec = pl.BlockSpec(memory_space=pl.ANY)
    grid_spec = pltpu.PrefetchScalarGridSpec(
        num_scalar_prefetch=1,
        grid=(n_seq,),
        in_specs=[pl.BlockSpec((1, tpad, D_MODEL), lambda b, pt: (b, 0, 0)),
                  pl.BlockSpec((1, tpad, D_MODEL), lambda b, pt: (b, 0, 0)),
                  pl.BlockSpec((1, tpad, D_MODEL), lambda b, pt: (b, 0, 0)),
                  pl.BlockSpec((1, FOX_HEADS, tpad), lambda b, pt: (b, 0, 0)),
                  any_spec, any_spec, any_spec],
        out_specs=pl.BlockSpec((1, tpad, D_MODEL), lambda b, pt: (b, 0, 0)),
        scratch_shapes=[pltpu.VMEM((2, D_MODEL, n_pages * PAGE), F32), pltpu.VMEM((2, D_MODEL, n_pages * PAGE), F32),
                        pltpu.VMEM((2, n_pages * FOX_HEADS, PAGE), F32), pltpu.SemaphoreType.DMA((2, 3))])
    return pl.pallas_call(
        functools.partial(_fox_sample_kernel, layer, t_new, n_pages),
        grid_spec=grid_spec,
        out_shape=jax.ShapeDtypeStruct((n_seq, tpad, D_MODEL), F32),
        compiler_params=_cparams(("arbitrary",)),
        name="fox_attn_sample",
    )(page_table.reshape(-1), q_new, k_new, v_new, nt, cache_kt, cache_vt, cache_lft)


def _valid_rows(tm, n_valid):
    row = pl.program_id(1) * tm + lax.broadcasted_iota(jnp.int32, (tm, 1), 0)
    return row < n_valid


def _fox_out_kernel(tm, n_valid, x_ref, o_ref, w_ref, g_ref, y_ref):
    m = _dot(o_ref[0].astype(BF16), w_ref[...])
    y_ref[0] = x_ref[0] + jnp.where(_valid_rows(tm, n_valid), _rms(m, g_ref[...]), 0.0)


def _fox_out(x, o, w, g, tm, n_valid):
    nb, length, _ = x.shape
    return pl.pallas_call(
        functools.partial(_fox_out_kernel, tm, n_valid),
        grid=(nb, length // tm),
        in_specs=[_row_spec(tm, D_MODEL), _row_spec(tm, D_MODEL),
                  _const_spec((D_MODEL, D_MODEL)), _const_spec((1, D_MODEL))],
        out_specs=_row_spec(tm, D_MODEL),
        out_shape=jax.ShapeDtypeStruct(x.shape, F32),
        compiler_params=_cparams(("parallel", "parallel")),
        name="fox_out",
    )(x, o, w, g)


def _gla_out_kernel(tm, n_valid, x_ref, o_ref, r_ref, gn_ref, w_ref, g_ref, y_ref):
    o = o_ref[0]
    r = r_ref[0]
    gn = gn_ref[...]
    parts = []
    for h in range(GLA_HEADS):
        cols = slice(h * GLA_DV_H, (h + 1) * GLA_DV_H)
        on = _rms(o[:, cols], gn[:, cols])
        rh = r[:, cols]
        parts.append((on * (rh * jax.nn.sigmoid(rh))).astype(BF16))
    m = _dot(jnp.concatenate(parts, axis=1), w_ref[...])
    y_ref[0] = x_ref[0] + jnp.where(_valid_rows(tm, n_valid), _rms(m, g_ref[...]), 0.0)


def _gla_out(x, o, r, gn, w, g, tm, n_valid):
    nb, length, _ = x.shape
    return pl.pallas_call(
        functools.partial(_gla_out_kernel, tm, n_valid),
        grid=(nb, length // tm),
        in_specs=[_row_spec(tm, D_MODEL), _row_spec(tm, GLA_DV), _row_spec(tm, GLA_DV),
                  _const_spec((1, GLA_DV)), _const_spec((GLA_DV, D_MODEL)), _const_spec((1, D_MODEL))],
        out_specs=_row_spec(tm, D_MODEL),
        out_shape=jax.ShapeDtypeStruct(x.shape, F32),
        compiler_params=_cparams(("parallel", "parallel")),
        name="gla_out",
    )(x, o, r, gn, w, g)


def _ffn_kernel(x_ref, g2_ref, g3_ref, wup_ref, wdn_ref, y_ref, xn_ref, acc_ref):
    x = x_ref[0]
    xn_ref[...] = _rms(x, g2_ref[...]).astype(BF16)
    acc_ref[...] = jnp.zeros(acc_ref.shape, F32)

    def chunk(c, _):
        xn = xn_ref[...]
        h1 = _dot(xn, wup_ref[c])
        h2 = _dot(xn, wup_ref[c + N_FF_CHUNKS])
        a = (h1 * jax.nn.sigmoid(h1) * h2).astype(BF16)
        acc_ref[...] += _dot(a, wdn_ref[c])
        return 0

    lax.fori_loop(0, N_FF_CHUNKS, chunk, 0)
    y_ref[0] = x + _rms(acc_ref[...], g3_ref[...])


def _ffn(x, g2, g3, wup, wdn, tm):
    nb, length, _ = x.shape
    return pl.pallas_call(
        _ffn_kernel,
        grid=(nb, length // tm),
        in_specs=[_row_spec(tm, D_MODEL), _const_spec((1, D_MODEL)), _const_spec((1, D_MODEL)),
                  _const_spec((2 * N_FF_CHUNKS, D_MODEL, FF_CHUNK)), _const_spec((N_FF_CHUNKS, FF_CHUNK, D_MODEL))],
        out_specs=_row_spec(tm, D_MODEL),
        out_shape=jax.ShapeDtypeStruct(x.shape, F32),
        scratch_shapes=[pltpu.VMEM((tm, D_MODEL), BF16), pltpu.VMEM((tm, D_MODEL), F32)],
        compiler_params=_cparams(("parallel", "parallel")),
        name="ffn",
    )(x, g2, g3, wup, wdn)


def _gla_proj_kernel(x_ref, g_ref, wq_ref, wk_ref, wv_ref, wr_ref, wl_ref, wg2_ref, bg_ref,
                     q_ref, k_ref, v_ref, r_ref, gate_ref):
    xn = _rms(x_ref[0], g_ref[...]).astype(BF16)
    q_ref[0] = _dot(xn, wq_ref[...]) * GLA_SCALE
    k_ref[0] = _dot(xn, wk_ref[...])
    v_ref[0] = _dot(xn, wv_ref[...]).astype(BF16)
    r_ref[0] = _dot(xn, wr_ref[...])
    low = _dot(xn, wl_ref[...]).astype(BF16)
    z = _dot(low, wg2_ref[...]) + bg_ref[...]
    gate_ref[0] = _log_sigmoid(z) * (1.0 / GLA_TAU)


def _gla_proj(x, g, wq, wk, wv, wr, wl, wg2, bg, tm):
    nb, length, _ = x.shape
    sds = jax.ShapeDtypeStruct
    return pl.pallas_call(
        _gla_proj_kernel,
        grid=(nb, length // tm),
        in_specs=[_row_spec(tm, D_MODEL), _const_spec((1, D_MODEL)),
                  _const_spec((D_MODEL, GLA_DK)), _const_spec((D_MODEL, GLA_DK)),
                  _const_spec((D_MODEL, GLA_DV)), _const_spec((D_MODEL, GLA_DV)),
                  _const_spec((D_MODEL, LANES)), _const_spec((LANES, GLA_DK)), _const_spec((1, GLA_DK))],
        out_specs=[_row_spec(tm, GLA_DK), _row_spec(tm, GLA_DK), _row_spec(tm, GLA_DV),
                   _row_spec(tm, GLA_DV), _row_spec(tm, GLA_DK)],
        out_shape=[sds((nb, length, GLA_DK), F32), sds((nb, length, GLA_DK), F32),
                   sds((nb, length, GLA_DV), BF16), sds((nb, length, GLA_DV), F32),
                   sds((nb, length, GLA_DK), F32)],
        compiler_params=_cparams(("parallel", "parallel")),
        name="gla_proj",
    )(x, g, wq, wk, wv, wr, wl, wg2, bg)


GLA_HPS = 2


def _gla_prompt_kernel(n_chunks, n_valid, q_ref, k_ref, g_ref, v_ref, o_ref, s_out_ref, s_ref):
    s_ref[...] = jnp.zeros(s_ref.shape, F32)
    r = lax.broadcasted_iota(jnp.int32, (GLA_CHUNK, GLA_CHUNK), 0)
    c = lax.broadcasted_iota(jnp.int32, (GLA_CHUNK, GLA_CHUNK), 1)
    causal = c <= r
    tri = causal.astype(F32)
    n_sub = GLA_CHUNK // GLA_SUB

    def chunk(ci, _):
        off = pl.multiple_of(ci * GLA_CHUNK, GLA_CHUNK)
        rows = pl.ds(off, GLA_CHUNK)
        pos = off + lax.broadcasted_iota(jnp.int32, (GLA_CHUNK, 1), 0)
        for hh in range(GLA_HPS):
            kcols = slice(hh * GLA_DK_H, (hh + 1) * GLA_DK_H)
            vcols = slice(hh * GLA_DV_H, (hh + 1) * GLA_DV_H)
            g = jnp.where(pos < n_valid, g_ref[0, rows, kcols], 0.0)
            q = q_ref[0, rows, kcols]
            k = k_ref[0, rows, kcols]
            v = v_ref[0, rows, vcols]
            big_g = _select_dot(tri, g)
            g_last = big_g[GLA_CHUNK - 1:GLA_CHUNK, :]
            g_last_col = big_g.T[:, GLA_CHUNK - 1:GLA_CHUNK]
            state = s_ref[hh]
            o = _dot((q * jnp.exp(big_g)).astype(BF16), state.astype(BF16))
            a_parts = []
            for i in range(n_sub):
                lo, hi = i * GLA_SUB, (i + 1) * GLA_SUB
                ref = big_g[lo - 1:lo, :] if i > 0 else jnp.zeros((1, GLA_DK_H), F32)
                q_i = (q[lo:hi, :] * jnp.exp(big_g[lo:hi, :] - ref)).astype(BF16)
                e = jnp.where(r < hi, jnp.exp(jnp.minimum(ref - big_g, EXP_CLAMP)), 0.0)
                a_parts.append(lax.dot_general(q_i, (k * e).astype(BF16), NT, preferred_element_type=F32))
            a = jnp.where(causal, jnp.concatenate(a_parts, axis=0), 0.0)
            o_ref[0, rows, vcols] = o + _dot(a.astype(BF16), v)
            k_dec = (k * jnp.exp(g_last - big_g)).T.astype(BF16)
            s_ref[hh] = state * jnp.exp(g_last_col) + _dot(k_dec, v)
        return 0

    lax.fori_loop(0, n_chunks, chunk, 0)
    s_out_ref[0] = s_ref[...]


def _gla_prompt(q, k, g, v, n_valid):
    nb, length, _ = q.shape
    sds = jax.ShapeDtypeStruct
    kspec = pl.BlockSpec((1, length, GLA_HPS * GLA_DK_H), lambda b, h: (b, 0, h))
    vspec = pl.BlockSpec((1, length, GLA_HPS * GLA_DV_H), lambda b, h: (b, 0, h))
    return pl.pallas_call(
        functools.partial(_gla_prompt_kernel, length // GLA_CHUNK, n_valid),
        grid=(nb, GLA_HEADS // GLA_HPS),
        in_specs=[kspec, kspec, kspec, vspec],
        out_specs=[vspec, pl.BlockSpec((1, GLA_HPS, GLA_DK_H, GLA_DV_H), lambda b, h: (b, h, 0, 0))],
        out_shape=[sds((nb, length, GLA_DV), F32), sds((nb, GLA_HEADS, GLA_DK_H, GLA_DV_H), F32)],
        scratch_shapes=[pltpu.VMEM((GLA_HPS, GLA_DK_H, GLA_DV_H), F32)],
        compiler_params=_cparams(("parallel", "parallel")),
        name="gla_prompt",
    )(q, k, g, v)


def _gla_sample_kernel(sb, t_new, q_ref, k_ref, g_ref, v_ref, s_in_ref, o_ref, s_out_ref):
    tpad = q_ref.shape[1]
    rowi = lax.broadcasted_iota(jnp.int32, (tpad, GLA_DK_H), 0)
    rowc = lax.broadcasted_iota(jnp.int32, (tpad, 1), 0)
    for i in range(sb):
        for h in range(GLA_HEADS):
            kcols = slice(h * GLA_DK_H, (h + 1) * GLA_DK_H)
            vcols = slice(h * GLA_DV_H, (h + 1) * GLA_DV_H)
            g = g_ref[i, :, kcols]
            q = q_ref[i, :, kcols]
            k = k_ref[i, :, kcols]
            v = v_ref[i, :, vcols]
            big_g = jnp.zeros((tpad, GLA_DK_H), F32)
            for u in range(t_new):
                big_g = big_g + jnp.where(rowi >= u, g[u:u + 1, :], 0.0)
            state = s_in_ref[0, i, h]
            o = _dot(q * jnp.exp(big_g), state)
            for s in range(t_new):
                w = jnp.exp(jnp.where(rowi >= s, big_g - big_g[s:s + 1, :], 0.0))
                a = jnp.sum(q * k[s:s + 1, :] * w, axis=-1, keepdims=True)
                o = o + jnp.where(rowc >= s, a, 0.0) * v[s:s + 1, :]
            o_ref[i, :, vcols] = o
            g_last = big_g[t_new - 1:t_new, :]
            k_dec = k * jnp.exp(g_last - big_g)
            g_last_col = jnp.broadcast_to(g_last, (tpad, GLA_DK_H)).T[:, 0:1]
            s_out_ref[0, i, h] = state * jnp.exp(g_last_col) + lax.dot_general(k_dec, v, TN, preferred_element_type=F32)


def _gla_sample(layer, t_new, q, k, g, v, state_all, sb):
    n_seq, tpad, _ = q.shape
    sds = jax.ShapeDtypeStruct
    kspec = pl.BlockSpec((sb, tpad, GLA_DK), lambda b: (b, 0, 0))
    vspec = pl.BlockSpec((sb, tpad, GLA_DV), lambda b: (b, 0, 0))
    sspec_in = pl.BlockSpec((1, sb, GLA_HEADS, GLA_DK_H, GLA_DV_H), lambda b: (layer, b, 0, 0, 0))
    sspec_out = pl.BlockSpec((1, sb, GLA_HEADS, GLA_DK_H, GLA_DV_H), lambda b: (0, b, 0, 0, 0))
    o, s_new = pl.pallas_call(
        functools.partial(_gla_sample_kernel, sb, t_new),
        grid=(n_seq // sb,),
        in_specs=[kspec, kspec, kspec, vspec, sspec_in],
        out_specs=[vspec, sspec_out],
        out_shape=[sds((n_seq, tpad, GLA_DV), F32), sds((1, n_seq, GLA_HEADS, GLA_DK_H, GLA_DV_H), F32)],
        compiler_params=_cparams(("parallel",)),
        name="gla_sample",
    )(q, k, g, v, state_all)
    return o, s_new[0]


def kernel(x_prompt, x_sample, cache_k, cache_v, cache_logf, state_gla, page_table, meta, norm_g, fox_w_in, fox_b_f, fox_w_out, gla_w_in, gla_w_gate2, gla_b_gate, gla_norm_g, gla_w_out, ffn_w_up, ffn_w_down):
    n_b, seq, _ = x_prompt.shape
    n_seq, t_new, _ = x_sample.shape
    depth = norm_g.shape[0]
    n_valid = N_META_TOK + seq
    length = -(-n_valid // LANES) * LANES
    tm_p = length // 4
    n_s_rows = n_seq * t_new
    dt = x_prompt.dtype

    xp = jnp.concatenate([jnp.broadcast_to(meta.astype(dt)[None], (n_b, N_META_TOK, D_MODEL)), x_prompt,
                          jnp.zeros((n_b, length - n_valid, D_MODEL), dt)], axis=1)
    xs = x_sample.reshape(1, n_s_rows, D_MODEL)
    tpad = 8

    def pad_rows(a):
        return jnp.pad(a.reshape(n_seq, t_new, a.shape[-1]), ((0, 0), (0, tpad - t_new), (0, 0)))

    n_pool = cache_k.shape[1]
    cache_kt = jnp.transpose(cache_k, (0, 1, 3, 4, 2)).reshape(cache_k.shape[0], n_pool, D_MODEL, PAGE)
    cache_vt = jnp.transpose(cache_v, (0, 1, 3, 4, 2)).reshape(cache_v.shape[0], n_pool, D_MODEL, PAGE)
    cache_lft = jnp.swapaxes(cache_logf, 2, 3)

    n_fox = (depth + 1) // 2
    kv_p = [jnp.zeros((n_fox, n_b, n_valid, D_MODEL), F32) for _ in range(2)]
    kv_s = [jnp.zeros((n_fox, 1, n_s_rows, D_MODEL), F32) for _ in range(2)]
    nf_p, ns_p, nf_s, ns_s = [], [], [], []
    for i in range(depth):
        g = norm_g[i].reshape(4, 1, D_MODEL)
        j = i // 2
        if i % 2 == 0:
            w_in = fox_w_in[j]
            wq = w_in[:, :D_MODEL].astype(BF16)
            wk = w_in[:, D_MODEL:2 * D_MODEL].astype(BF16)
            wv = w_in[:, 2 * D_MODEL:3 * D_MODEL].astype(BF16)
            wf = jnp.pad(w_in[:, 3 * D_MODEL:], ((0, 0), (0, LANES - FOX_HEADS))).astype(BF16)
            bf = fox_b_f[j].reshape(1, FOX_HEADS)
            w_out = fox_w_out[j].astype(BF16)
            q16, *kv_p, k16, v16, lf = _fox_proj(xp, g[0], wq, wk, wv, wf, bf, tm_p, j, *kv_p)
            ccol = _fox_cumsum(lf)
            o16 = _fox_attn(q16, k16, v16, ccol)
            xp = _fox_out(xp, o16, w_out, g[1], tm_p, n_valid)
            nf_p.append(lf[:, :n_valid])
            q16s, *kv_s, _, _, lfs = _fox_proj(xs, g[0], wq, wk, wv, wf, bf, n_s_rows, j, *kv_s)
            k32s, v32s = kv_s[0][j], kv_s[1][j]
            nt = _fox_new_bias(lfs[0], t_new)
            nt = jnp.transpose(nt.reshape(FOX_HEADS, n_seq, t_new), (1, 0, 2))
            nt = jnp.pad(nt, ((0, 0), (0, 0), (0, tpad - t_new)))
            os_ = _fox_sample(j, t_new, page_table, pad_rows(q16s.astype(F32)), pad_rows(k32s), pad_rows(v32s), nt,
                              cache_kt, cache_vt, cache_lft)
            xs = _fox_out(xs, os_[:, :t_new].reshape(1, n_s_rows, D_MODEL), w_out, g[1], n_s_rows, n_s_rows)
            nf_s.append(lfs.reshape(n_seq, t_new, FOX_HEADS))
        else:
            w_in = gla_w_in[j]
            o1, o2, o3, o4 = GLA_DK, 2 * GLA_DK, 2 * GLA_DK + GLA_DV, 2 * GLA_DK + 2 * GLA_DV
            wq = w_in[:, :o1].astype(BF16)
            wk = w_in[:, o1:o2].astype(BF16)
            wv = w_in[:, o2:o3].astype(BF16)
            wr = w_in[:, o3:o4].astype(BF16)
            wl = jnp.pad(w_in[:, o4:], ((0, 0), (0, LANES - GLA_RANK))).astype(BF16)
            wg2 = jnp.pad(gla_w_gate2[j], ((0, LANES - GLA_RANK), (0, 0))).astype(BF16)
            bg = gla_b_gate[j].reshape(1, GLA_DK)
            gn = gla_norm_g[j].reshape(1, GLA_DV)
            w_out = gla_w_out[j].astype(BF16)
            q, k, v, r, gate = _gla_proj(xp, g[0], wq, wk, wv, wr, wl, wg2, bg, tm_p)
            o, s_fin = _gla_prompt(q, k, gate, v, n_valid)
            xp = _gla_out(xp, o, r, gn, w_out, g[1], tm_p, n_valid)
            ns_p.append(s_fin)
            q, k, v, r, gate = _gla_proj(xs, g[0], wq, wk, wv, wr, wl, wg2, bg, n_s_rows)
            o, s_new = _gla_sample(j, t_new, pad_rows(q), pad_rows(k), pad_rows(gate), pad_rows(v.astype(F32)),
                                   state_gla, 4)
            xs = _gla_out(xs, o[:, :t_new].reshape(1, n_s_rows, GLA_DV), r, gn, w_out, g[1], n_s_rows, n_s_rows)
            ns_s.append(s_new)
        wup = jnp.transpose(ffn_w_up[i].astype(BF16).reshape(D_MODEL, 2 * N_FF_CHUNKS, FF_CHUNK), (1, 0, 2))
        wdn = ffn_w_down[i].astype(BF16).reshape(N_FF_CHUNKS, FF_CHUNK, D_MODEL)
        xp = _ffn(xp, g[2], g[3], wup, wdn, 2 * tm_p)
        xs = _ffn(xs, g[2], g[3], wup, wdn, n_s_rows)

    y_prompt = xp[:, N_META_TOK:n_valid]
    y_sample = xs.reshape(n_seq, t_new, D_MODEL)
    heads = lambda a, rows: a.reshape(n_fox, -1, rows, FOX_HEADS, FOX_HEAD_DIM)
    return (y_prompt, y_sample,
            heads(kv_p[0], n_valid), heads(kv_p[1], n_valid), jnp.stack(nf_p), jnp.stack(ns_p),
            heads(kv_s[0], t_new), heads(kv_s[1], t_new), jnp.stack(nf_s), jnp.stack(ns_s))
```

```python
import functools

import jax
import jax.numpy as jnp
from jax import lax
from jax.experimental import pallas as pl
from jax.experimental.pallas import tpu as pltpu

D_MODEL = 1024
N_META_TOK = 16
FOX_HEADS = 16
FOX_HEAD_DIM = 64
FOX_SCALE = FOX_HEAD_DIM ** -0.5
GLA_HEADS = 4
GLA_DK = 512
GLA_DV = 1024
GLA_DK_H = 128
GLA_DV_H = 256
GLA_SCALE = GLA_DK_H ** -0.5
GLA_RANK = 16
GLA_TAU = 16.0
D_FF = 2816
RMS_EPS = 1e-6
NEG = -1e30
PAGE = 128

LANES = 128
FF_CHUNK = 256
N_FF_CHUNKS = D_FF // FF_CHUNK
GLA_CHUNK = 128
GLA_SUB = 16
EXP_CLAMP = 80.0
VMEM_LIMIT = 56 * 1024 * 1024

F32 = jnp.float32
BF16 = jnp.bfloat16
NT = (((1,), (1,)), ((), ()))
TN = (((0,), (0,)), ((), ()))


def _cparams(sem):
    return pltpu.CompilerParams(dimension_semantics=sem, vmem_limit_bytes=VMEM_LIMIT)


def _rms(x, g):
    return x * lax.rsqrt(jnp.mean(x * x, axis=-1, keepdims=True) + RMS_EPS) * g


def _dot(a, b):
    return jnp.dot(a, b, preferred_element_type=F32)


def _select_dot(sel, x, dims=None, sel_first=True):
    hi = x.astype(BF16)
    r1 = x - hi.astype(F32)
    mid = r1.astype(BF16)
    lo = (r1 - mid.astype(F32)).astype(BF16)
    sel = sel.astype(BF16)
    out = None
    for piece in (hi, mid, lo):
        a, b = (sel, piece) if sel_first else (piece, sel)
        d = _dot(a, b) if dims is None else lax.dot_general(a, b, dims, preferred_element_type=F32)
        out = d if out is None else out + d
    return out


def _log_sigmoid(z):
    return jnp.minimum(z, 0.0) - jnp.log1p(jnp.exp(-jnp.abs(z)))


def _const_spec(shape):
    n = len(shape)
    return pl.BlockSpec(shape, lambda *_: (0,) * n, pipeline_mode=pl.Buffered(1))


def _row_spec(tm, width):
    return pl.BlockSpec((1, tm, width), lambda b, i: (b, i, 0))


def _fox_proj_kernel(q_scale, x_ref, g_ref, wq_ref, wk_ref, wv_ref, wf_ref, bf_ref, k_all_ref, v_all_ref,
                     q16_ref, k_ref, v_ref, k16_ref, v16_ref, lf_ref):
    del k_all_ref, v_all_ref
    xn = _rms(x_ref[0], g_ref[...]).astype(BF16)
    q16_ref[0] = (_dot(xn, wq_ref[...]) * q_scale).astype(BF16)
    k = _dot(xn, wk_ref[...])
    k_ref[0] = k
    k16_ref[0] = k.astype(BF16)
    v = _dot(xn, wv_ref[...])
    v_ref[0] = v
    v16_ref[0] = v.astype(BF16)
    z = _dot(xn, wf_ref[...])[:, :FOX_HEADS] + bf_ref[...]
    lf_ref[0] = _log_sigmoid(z)


def _fox_proj(x, g, wq, wk, wv, wf, bf, tm, q_scale, layer, k_all, v_all):
    nb, length, _ = x.shape
    sds = jax.ShapeDtypeStruct
    slab_spec = pl.BlockSpec((None, 1, tm, D_MODEL), lambda b, i: (layer, b, i, 0))
    slab_shape = sds(k_all.shape, F32)
    any_spec = pl.BlockSpec(memory_space=pl.ANY)
    n_in = 7
    return pl.pallas_call(
        functools.partial(_fox_proj_kernel, q_scale),
        grid=(nb, length // tm),
        in_specs=[_row_spec(tm, D_MODEL), _const_spec((1, D_MODEL)),
                  _const_spec((D_MODEL, D_MODEL)), _const_spec((D_MODEL, D_MODEL)),
                  _const_spec((D_MODEL, D_MODEL)), _const_spec((D_MODEL, LANES)),
                  _const_spec((1, FOX_HEADS)), any_spec, any_spec],
        out_specs=[_row_spec(tm, D_MODEL), slab_spec, slab_spec, _row_spec(tm, D_MODEL), _row_spec(tm, D_MODEL),
                   _row_spec(tm, FOX_HEADS)],
        out_shape=[sds((nb, length, D_MODEL), BF16), slab_shape, slab_shape, sds((nb, length, D_MODEL), BF16),
                   sds((nb, length, D_MODEL), BF16), sds((nb, length, FOX_HEADS), F32)],
        input_output_aliases={n_in: 1, n_in + 1: 2},
        compiler_params=_cparams(("parallel", "parallel")),
        name="fox_proj",
    )(x, g, wq, wk, wv, wf, bf, k_all, v_all)


N_BIAS = 3
LOG2E = 1.4426950408889634


def _fox_cumsum_kernel(nblk, lf_ref, c3_ref):
    r = lax.broadcasted_iota(jnp.int32, (LANES, LANES), 0)
    c = lax.broadcasted_iota(jnp.int32, (LANES, LANES), 1)
    tri = (c <= r).astype(F32)
    hrow = lax.broadcasted_iota(jnp.int32, (FOX_HEADS, LANES), 0)
    lane = lax.broadcasted_iota(jnp.int32, (FOX_HEADS, LANES), 1)
    carry = jnp.zeros((1, FOX_HEADS), F32)
    for b in range(nblk):
        rows = slice(b * LANES, (b + 1) * LANES)
        cb = _select_dot(tri, lf_ref[0, rows, :]) + carry
        carry = cb[LANES - 1:LANES, :]
        rest = cb * LOG2E
        placed = None
        for j in range(N_BIAS):
            piece = rest.astype(BF16)
            rest = rest - piece.astype(F32)
            d = _dot(piece, (lane == hrow + FOX_HEADS * j).astype(BF16))
            placed = d if placed is None else placed + d
        c3_ref[0, rows, :] = placed.astype(BF16)


def _fox_cumsum(lf):
    nb, length, _ = lf.shape
    return pl.pallas_call(
        functools.partial(_fox_cumsum_kernel, length // LANES),
        grid=(nb,),
        in_specs=[pl.BlockSpec((1, length, FOX_HEADS), lambda b: (b, 0, 0))],
        out_specs=pl.BlockSpec((1, length, LANES), lambda b: (b, 0, 0)),
        out_shape=jax.ShapeDtypeStruct((nb, length, LANES), BF16),
        compiler_params=_cparams(("parallel",)),
        name="fox_cumsum",
    )(lf)


ATT_TQ = 256


def _fox_attn_kernel(q_ref, k_ref, v_ref, c_ref, o_ref, qa_ref, ka_ref, vt_ref, s_ref):
    hp = pl.program_id(1)
    length = q_ref.shape[1]
    half = FOX_HEAD_DIM

    prow = lax.broadcasted_iota(jnp.int32, (LANES, 2 * LANES), 0)
    col = lax.broadcasted_iota(jnp.int32, (LANES, 2 * LANES), 1)
    piece = prow // FOX_HEADS
    hrow = prow % FOX_HEADS
    is_k = col >= LANES
    lane_in = col - jnp.where(is_k, LANES, 0)
    start = jnp.where(hrow == 2 * hp, half, jnp.where(hrow == 2 * hp + 1, 0, -LANES)) + jnp.where(is_k, N_BIAS, 0)
    sel = jnp.where((piece < N_BIAS) & (lane_in == start + piece), jnp.where(is_k, -1.0, 1.0), 0.0)
    ext = _dot(c_ref[0], sel.astype(BF16))
    lane1 = lax.broadcasted_iota(jnp.int32, (1, LANES), 1)
    within = lane1 % half
    ones_q = ((within >= N_BIAS) & (within < 2 * N_BIAS)).astype(F32)
    ones_k = (within < N_BIAS).astype(F32)
    q_ext = (ext[:, :LANES] + ones_q).astype(BF16)
    k_ext = (ext[:, LANES:] + ones_k).astype(BF16)
    first = lane1 < half
    q2 = q_ref[0]
    k2 = k_ref[0]
    qa_ref[0] = jnp.where(first, q2, q_ext)
    qa_ref[1] = jnp.where(first, q_ext, q2)
    ka_ref[0] = jnp.where(first, k2, k_ext)
    ka_ref[1] = jnp.where(first, k_ext, k2)
    for b in range(length // LANES):
        rows = slice(b * LANES, (b + 1) * LANES)
        vt_ref[:, rows] = v_ref[0, rows, :].astype(F32).T.astype(BF16)

    def block(r0, tq, n_chunks):
        kk = lax.broadcasted_iota(jnp.int32, (tq, tq), 0)
        qq = lax.broadcasted_iota(jnp.int32, (tq, tq), 1)
        visible = kk <= qq
        qa = [qa_ref[i, pl.ds(r0, tq), :] for i in range(2)]
        vt_rows = [slice(i * half, (i + 1) * half) for i in range(2)]

        def sweep(fn, carry):
            n4 = n_chunks // 4
            n2 = (n_chunks - 4 * n4) // 2
            carry = lax.fori_loop(0, n4, lambda j4, c: fn(4 * j4, 4, c), carry)
            carry = lax.fori_loop(0, n2, lambda j2, c: fn(4 * n4 + 2 * j2, 2, c), carry)
            return lax.fori_loop(4 * n4 + 2 * n2, n_chunks, lambda j, c: fn(j, 1, c), carry)

        def scores(j, n, ms):
            koff = pl.multiple_of(j * ATT_TQ, ATT_TQ)
            out = []
            for i in range(2):
                s = lax.dot_general(ka_ref[i, pl.ds(koff, n * ATT_TQ), :], qa[i], NT, preferred_element_type=F32)
                s_ref[i, pl.ds(koff, n * ATT_TQ), 0:tq] = s
                out.append(jnp.maximum(ms[i], jnp.max(s, axis=0, keepdims=True)))
            return tuple(out)

        sd = [jnp.where(visible, lax.dot_general(ka_ref[i, pl.ds(r0, tq), :], qa[i], NT, preferred_element_type=F32),
                        NEG) for i in range(2)]
        ms = sweep(scores, tuple(jnp.max(sd[i], axis=0, keepdims=True) for i in range(2)))
        init = []
        for i in range(2):
            pd = jnp.exp2(sd[i] - ms[i])
            init += [jnp.sum(pd, axis=0, keepdims=True), _dot(vt_ref[vt_rows[i], pl.ds(r0, tq)], pd.astype(BF16))]

        def weighted(j, n, carry):
            out = list(carry)
            for u in range(n):
                koff = pl.multiple_of((j + u) * ATT_TQ, ATT_TQ)
                for i in range(2):
                    p = jnp.exp2(s_ref[i, pl.ds(koff, ATT_TQ), 0:tq] - ms[i])
                    out[2 * i] = out[2 * i] + jnp.sum(p, axis=0, keepdims=True)
                    out[2 * i + 1] = out[2 * i + 1] + _dot(vt_ref[vt_rows[i], pl.ds(koff, ATT_TQ)], p.astype(BF16))
            return tuple(out)

        la, acca, lb, accb = sweep(weighted, tuple(init))
        o_ref[0, pl.ds(r0, tq), :] = jnp.concatenate([acca / la, accb / lb], axis=0).T.astype(BF16)

    n_blocks = length // ATT_TQ

    def body(qb, _):
        block(pl.multiple_of(qb * ATT_TQ, ATT_TQ), ATT_TQ, qb)
        return 0

    lax.fori_loop(0, n_blocks, body, 0)
    tail = length - n_blocks * ATT_TQ
    if tail:
        block(n_blocks * ATT_TQ, tail, n_blocks)


def _fox_attn(q16, k16, v16, c3):
    nb, length, _ = q16.shape
    n_hp = FOX_HEADS // 2
    col_spec = pl.BlockSpec((1, length, LANES), lambda b, h: (b, 0, h))
    return pl.pallas_call(
        _fox_attn_kernel,
        grid=(nb, n_hp),
        in_specs=[col_spec, col_spec, col_spec,
                  pl.BlockSpec((1, length, LANES), lambda b, h: (b, 0, 0))],
        out_specs=col_spec,
        out_shape=jax.ShapeDtypeStruct((nb, length, D_MODEL), BF16),
        scratch_shapes=[pltpu.VMEM((2, length, LANES), BF16), pltpu.VMEM((2, length, LANES), BF16),
                        pltpu.VMEM((LANES, length), BF16),
                        pltpu.VMEM((2, length // ATT_TQ * ATT_TQ, ATT_TQ), F32)],
        compiler_params=_cparams(("parallel", "parallel")),
        name="fox_attn_prompt",
    )(q16, k16, v16, c3)


def _fox_new_bias_kernel(t_new, lf_ref, nt_ref):
    n = lf_ref.shape[0]
    r = lax.broadcasted_iota(jnp.int32, (n, n), 0)
    c = lax.broadcasted_iota(jnp.int32, (n, n), 1)
    same_seq = (r // t_new) == (c // t_new)
    bd = (same_seq & (c <= r)).astype(F32)
    cum = _select_dot(bd, lf_ref[...])
    er = lax.broadcasted_iota(jnp.int32, (FOX_HEADS, FOX_HEADS), 0)
    ec = lax.broadcasted_iota(jnp.int32, (FOX_HEADS, FOX_HEADS), 1)
    eye = (er == ec).astype(F32)
    nt_ref[...] = _select_dot(eye, cum, NT)


def _fox_new_bias(lf, t_new):
    n = lf.shape[0]
    return pl.pallas_call(
        functools.partial(_fox_new_bias_kernel, t_new),
        out_shape=jax.ShapeDtypeStruct((FOX_HEADS, n), F32),
        compiler_params=pltpu.CompilerParams(vmem_limit_bytes=VMEM_LIMIT),
        name="fox_new_bias",
    )(lf)


def _fox_sample_kernel(layer, t_new, n_pages, pt_ref, q_ref, kn_ref, vn_ref, nt_ref, ck_hbm, cv_hbm, clf_hbm,
                       o_ref, kbuf, vbuf, lfbuf, sems):
    b = pl.program_id(0)
    n_seq = pl.num_programs(0)
    rows = t_new * FOX_HEADS

    def page_copies(seq, slot):
        copies = []
        for p in range(n_pages):
            page = pt_ref[seq * n_pages + p]
            copies.append(pltpu.make_async_copy(ck_hbm.at[layer, page], kbuf.at[slot, :, pl.ds(p * PAGE, PAGE)],
                                                sems.at[slot, 0]))
            copies.append(pltpu.make_async_copy(cv_hbm.at[layer, page], vbuf.at[slot, :, pl.ds(p * PAGE, PAGE)],
                                                sems.at[slot, 1]))
            copies.append(pltpu.make_async_copy(clf_hbm.at[layer, page],
                                                lfbuf.at[slot, pl.ds(p * FOX_HEADS, FOX_HEADS)], sems.at[slot, 2]))
        return copies

    slot = b % 2

    @pl.when(b == 0)
    def _():
        for cp in page_copies(0, 0):
            cp.start()

    @pl.when(b + 1 < n_seq)
    def _():
        for cp in page_copies(b + 1, 1 - slot):
            cp.start()

    row = lax.broadcasted_iota(jnp.int32, (rows, D_MODEL), 0)
    lane = lax.broadcasted_iota(jnp.int32, (rows, D_MODEL), 1)
    own_head = (row % FOX_HEADS) == (lane // FOX_HEAD_DIM)
    q = q_ref[0]
    qrep = jnp.concatenate([jnp.broadcast_to(q[t:t + 1, :], (FOX_HEADS, D_MODEL)) for t in range(t_new)], axis=0)
    qexp = jnp.where(own_head, qrep, 0.0).astype(BF16)
    qf = qexp.astype(F32)
    kn = kn_ref[0].astype(BF16).astype(F32)
    vn = vn_ref[0]
    nt = nt_ref[0]
    row_t = lax.broadcasted_iota(jnp.int32, (rows, 1), 0) // FOX_HEADS
    cols = []
    for t2 in range(t_new):
        sc = jnp.sum(qf * kn[t2:t2 + 1, :], axis=-1, keepdims=True)
        sc = sc - jnp.concatenate([nt[:, t2:t2 + 1]] * t_new, axis=0)
        cols.append(jnp.where(row_t >= t2, sc, NEG))
    m_new = functools.reduce(jnp.maximum, cols)

    for cp in page_copies(b, slot):
        cp.wait()

    lft = lfbuf[slot]
    j = lax.broadcasted_iota(jnp.int32, (PAGE, PAGE), 0)
    key = lax.broadcasted_iota(jnp.int32, (PAGE, PAGE), 1)
    within = _select_dot((j > key).astype(F32), lft, sel_first=False)
    total = jnp.broadcast_to(within[:, 0:1] + lft[:, 0:1], (n_pages * FOX_HEADS, PAGE))
    pr = lax.broadcasted_iota(jnp.int32, (n_pages * FOX_HEADS, n_pages * FOX_HEADS), 0)
    pc = lax.broadcasted_iota(jnp.int32, (n_pages * FOX_HEADS, n_pages * FOX_HEADS), 1)
    later_page = ((pr % FOX_HEADS) == (pc % FOX_HEADS)) & (pc // FOX_HEADS > pr // FOX_HEADS)
    suf = within + _select_dot(later_page.astype(F32), total)
    bias = jnp.concatenate(
        [jnp.concatenate([suf[p * FOX_HEADS:(p + 1) * FOX_HEADS, :]] * t_new, axis=0) for p in range(n_pages)], axis=1)

    s = _dot(qf, kbuf[slot]) + bias
    m = jnp.maximum(m_new, jnp.max(s, axis=-1, keepdims=True))
    p_past = jnp.exp(s - m)
    p_new = [jnp.exp(sc - m) for sc in cols]
    l = jnp.sum(p_past, axis=-1, keepdims=True) + functools.reduce(jnp.add, p_new)
    acc = _dot(vbuf[slot], p_past.T).T
    acc = acc + functools.reduce(jnp.add, [p_new[t2] * vn[t2:t2 + 1, :] for t2 in range(t_new)])
    o = jnp.where(own_head, acc / l, 0.0)
    o_ref[0, 0:t_new, :] = o.reshape(t_new, FOX_HEADS, D_MODEL).sum(axis=1)
    o_ref[0, t_new:, :] = jnp.zeros((o_ref.shape[1] - t_new, D_MODEL), F32)


def _fox_sample(layer, t_new, page_table, q_new, k_new, v_new, nt, cache_kt, cache_vt, cache_lft):
    n_seq, tpad, _ = q_new.shape
    n_pages = page_table.shape[1]
    any_spec = pl.BlockSpec(memory_space=pl.ANY)
    grid_spec = pltpu.PrefetchScalarGridSpec(
        num_scalar_prefetch=1,
        grid=(n_seq,),
        in_specs=[pl.BlockSpec((1, tpad, D_MODEL), lambda b, pt: (b, 0, 0)),
                  pl.BlockSpec((1, tpad, D_MODEL), lambda b, pt: (b, 0, 0)),
                  pl.BlockSpec((1, tpad, D_MODEL), lambda b, pt: (b, 0, 0)),
                  pl.BlockSpec((1, FOX_HEADS, tpad), lambda b, pt: (b, 0, 0)),
                  any_spec, any_spec, any_spec],
        out_specs=pl.BlockSpec((1, tpad, D_MODEL), lambda b, pt: (b, 0, 0)),
        scratch_shapes=[pltpu.VMEM((2, D_MODEL, n_pages * PAGE), F32), pltpu.VMEM((2, D_MODEL, n_pages * PAGE), F32),
                        pltpu.VMEM((2, n_pages * FOX_HEADS, PAGE), F32), pltpu.SemaphoreType.DMA((2, 3))])
    return pl.pallas_call(
        functools.partial(_fox_sample_kernel, layer, t_new, n_pages),
        grid_spec=grid_spec,
        out_shape=jax.ShapeDtypeStruct((n_seq, tpad, D_MODEL), F32),
        compiler_params=_cparams(("arbitrary",)),
        name="fox_attn_sample",
    )(page_table.reshape(-1), q_new, k_new, v_new, nt, cache_kt, cache_vt, cache_lft)


def _valid_rows(tm, n_valid):
    row = pl.program_id(1) * tm + lax.broadcasted_iota(jnp.int32, (tm, 1), 0)
    return row < n_valid


def _fox_out_kernel(tm, n_valid, x_ref, o_ref, w_ref, g_ref, y_ref):
    m = _dot(o_ref[0].astype(BF16), w_ref[...])
    y_ref[0] = x_ref[0] + jnp.where(_valid_rows(tm, n_valid), _rms(m, g_ref[...]), 0.0)


def _fox_out(x, o, w, g, tm, n_valid):
    nb, length, _ = x.shape
    return pl.pallas_call(
        functools.partial(_fox_out_kernel, tm, n_valid),
        grid=(nb, length // tm),
        in_specs=[_row_spec(tm, D_MODEL), _row_spec(tm, D_MODEL),
                  _const_spec((D_MODEL, D_MODEL)), _const_spec((1, D_MODEL))],
        out_specs=_row_spec(tm, D_MODEL),
        out_shape=jax.ShapeDtypeStruct(x.shape, F32),
        compiler_params=_cparams(("parallel", "parallel")),
        name="fox_out",
    )(x, o, w, g)


def _gla_out_kernel(tm, n_valid, x_ref, o_ref, r_ref, gn_ref, w_ref, g_ref, y_ref):
    o = o_ref[0]
    r = r_ref[0]
    gn = gn_ref[...]
    parts = []
    for h in range(GLA_HEADS):
        cols = slice(h * GLA_DV_H, (h + 1) * GLA_DV_H)
        on = _rms(o[:, cols], gn[:, cols])
        rh = r[:, cols]
        parts.append((on * (rh * jax.nn.sigmoid(rh))).astype(BF16))
    m = _dot(jnp.concatenate(parts, axis=1), w_ref[...])
    y_ref[0] = x_ref[0] + jnp.where(_valid_rows(tm, n_valid), _rms(m, g_ref[...]), 0.0)


def _gla_out(x, o, r, gn, w, g, tm, n_valid):
    nb, length, _ = x.shape
    return pl.pallas_call(
        functools.partial(_gla_out_kernel, tm, n_valid),
        grid=(nb, length // tm),
        in_specs=[_row_spec(tm, D_MODEL), _row_spec(tm, GLA_DV), _row_spec(tm, GLA_DV),
                  _const_spec((1, GLA_DV)), _const_spec((GLA_DV, D_MODEL)), _const_spec((1, D_MODEL))],
        out_specs=_row_spec(tm, D_MODEL),
        out_shape=jax.ShapeDtypeStruct(x.shape, F32),
        compiler_params=_cparams(("parallel", "parallel")),
        name="gla_out",
    )(x, o, r, gn, w, g)


def _ffn_kernel(x_ref, g2_ref, g3_ref, wup_ref, wdn_ref, y_ref, xn_ref, acc_ref):
    x = x_ref[0]
    xn_ref[...] = _rms(x, g2_ref[...]).astype(BF16)
    acc_ref[...] = jnp.zeros(acc_ref.shape, F32)

    def chunk(c, _):
        xn = xn_ref[...]
        col = pl.multiple_of(c * FF_CHUNK, FF_CHUNK)
        h1 = _dot(xn, wup_ref[:, pl.ds(col, FF_CHUNK)])
        h2 = _dot(xn, wup_ref[:, pl.ds(D_FF + col, FF_CHUNK)])
        a = (h1 * jax.nn.sigmoid(h1) * h2).astype(BF16)
        acc_ref[...] += _dot(a, wdn_ref[c])
        return 0

    lax.fori_loop(0, N_FF_CHUNKS, chunk, 0)
    y_ref[0] = x + _rms(acc_ref[...], g3_ref[...])


def _ffn(x, g2, g3, wup, wdn, tm):
    nb, length, _ = x.shape
    return pl.pallas_call(
        _ffn_kernel,
        grid=(nb, length // tm),
        in_specs=[_row_spec(tm, D_MODEL), _const_spec((1, D_MODEL)), _const_spec((1, D_MODEL)),
                  _const_spec((D_MODEL, 2 * D_FF)), _const_spec((N_FF_CHUNKS, FF_CHUNK, D_MODEL))],
        out_specs=_row_spec(tm, D_MODEL),
        out_shape=jax.ShapeDtypeStruct(x.shape, F32),
        scratch_shapes=[pltpu.VMEM((tm, D_MODEL), BF16), pltpu.VMEM((tm, D_MODEL), F32)],
        compiler_params=_cparams(("parallel", "parallel")),
        name="ffn",
    )(x, g2, g3, wup, wdn)


def _gla_proj_kernel(x_ref, g_ref, wq_ref, wk_ref, wv_ref, wr_ref, wl_ref, wg2_ref, bg_ref,
                     q_ref, k_ref, v_ref, r_ref, gate_ref):
    xn = _rms(x_ref[0], g_ref[...]).astype(BF16)
    q_ref[0] = _dot(xn, wq_ref[...]) * GLA_SCALE
    k_ref[0] = _dot(xn, wk_ref[...])
    v_ref[0] = _dot(xn, wv_ref[...]).astype(BF16)
    r_ref[0] = _dot(xn, wr_ref[...])
    low = _dot(xn, wl_ref[...]).astype(BF16)
    z = _dot(low, wg2_ref[...]) + bg_ref[...]
    gate_ref[0] = _log_sigmoid(z) * (1.0 / GLA_TAU)


def _gla_proj(x, g, wq, wk, wv, wr, wl, wg2, bg, tm):
    nb, length, _ = x.shape
    sds = jax.ShapeDtypeStruct
    return pl.pallas_call(
        _gla_proj_kernel,
        grid=(nb, length // tm),
        in_specs=[_row_spec(tm, D_MODEL), _const_spec((1, D_MODEL)),
                  _const_spec((D_MODEL, GLA_DK)), _const_spec((D_MODEL, GLA_DK)),
                  _const_spec((D_MODEL, GLA_DV)), _const_spec((D_MODEL, GLA_DV)),
                  _const_spec((D_MODEL, LANES)), _const_spec((LANES, GLA_DK)), _const_spec((1, GLA_DK))],
        out_specs=[_row_spec(tm, GLA_DK), _row_spec(tm, GLA_DK), _row_spec(tm, GLA_DV),
                   _row_spec(tm, GLA_DV), _row_spec(tm, GLA_DK)],
        out_shape=[sds((nb, length, GLA_DK), F32), sds((nb, length, GLA_DK), F32),
                   sds((nb, length, GLA_DV), BF16), sds((nb, length, GLA_DV), F32),
                   sds((nb, length, GLA_DK), F32)],
        compiler_params=_cparams(("parallel", "parallel")),
        name="gla_proj",
    )(x, g, wq, wk, wv, wr, wl, wg2, bg)


GLA_HPS = 2


def _gla_prompt_kernel(n_chunks, n_valid, q_ref, k_ref, g_ref, v_ref, o_ref, s_out_ref, s_ref):
    s_ref[...] = jnp.zeros(s_ref.shape, F32)
    r = lax.broadcasted_iota(jnp.int32, (GLA_CHUNK, GLA_CHUNK), 0)
    c = lax.broadcasted_iota(jnp.int32, (GLA_CHUNK, GLA_CHUNK), 1)
    causal = c <= r
    tri = causal.astype(F32)
    n_sub = GLA_CHUNK // GLA_SUB

    def chunk(ci, _):
        off = pl.multiple_of(ci * GLA_CHUNK, GLA_CHUNK)
        rows = pl.ds(off, GLA_CHUNK)
        pos = off + lax.broadcasted_iota(jnp.int32, (GLA_CHUNK, 1), 0)
        for hh in range(GLA_HPS):
            kcols = slice(hh * GLA_DK_H, (hh + 1) * GLA_DK_H)
            vcols = slice(hh * GLA_DV_H, (hh + 1) * GLA_DV_H)
            g = jnp.where(pos < n_valid, g_ref[0, rows, kcols], 0.0)
            q = q_ref[0, rows, kcols]
            k = k_ref[0, rows, kcols]
            v = v_ref[0, rows, vcols]
            big_g = _select_dot(tri, g)
            g_last = big_g[GLA_CHUNK - 1:GLA_CHUNK, :]
            g_last_col = big_g.T[:, GLA_CHUNK - 1:GLA_CHUNK]
            state = s_ref[hh]
            o = _dot((q * jnp.exp(big_g)).astype(BF16), state.astype(BF16))
            a_parts = []
            for i in range(n_sub):
                lo, hi = i * GLA_SUB, (i + 1) * GLA_SUB
                ref = big_g[lo - 1:lo, :] if i > 0 else jnp.zeros((1, GLA_DK_H), F32)
                q_i = (q[lo:hi, :] * jnp.exp(big_g[lo:hi, :] - ref)).astype(BF16)
                e = jnp.where(r < hi, jnp.exp(jnp.minimum(ref - big_g, EXP_CLAMP)), 0.0)
                a_parts.append(lax.dot_general(q_i, (k * e).astype(BF16), NT, preferred_element_type=F32))
            a = jnp.where(causal, jnp.concatenate(a_parts, axis=0), 0.0)
            o_ref[0, rows, vcols] = o + _dot(a.astype(BF16), v)
            k_dec = (k * jnp.exp(g_last - big_g)).T.astype(BF16)
            s_ref[hh] = state * jnp.exp(g_last_col) + _dot(k_dec, v)
        return 0

    lax.fori_loop(0, n_chunks, chunk, 0)
    s_out_ref[0] = s_ref[...]


def _gla_prompt(q, k, g, v, n_valid):
    nb, length, _ = q.shape
    sds = jax.ShapeDtypeStruct
    kspec = pl.BlockSpec((1, length, GLA_HPS * GLA_DK_H), lambda b, h: (b, 0, h))
    vspec = pl.BlockSpec((1, length, GLA_HPS * GLA_DV_H), lambda b, h: (b, 0, h))
    return pl.pallas_call(
        functools.partial(_gla_prompt_kernel, length // GLA_CHUNK, n_valid),
        grid=(nb, GLA_HEADS // GLA_HPS),
        in_specs=[kspec, kspec, kspec, vspec],
        out_specs=[vspec, pl.BlockSpec((1, GLA_HPS, GLA_DK_H, GLA_DV_H), lambda b, h: (b, h, 0, 0))],
        out_shape=[sds((nb, length, GLA_DV), F32), sds((nb, GLA_HEADS, GLA_DK_H, GLA_DV_H), F32)],
        scratch_shapes=[pltpu.VMEM((GLA_HPS, GLA_DK_H, GLA_DV_H), F32)],
        compiler_params=_cparams(("parallel", "parallel")),
        name="gla_prompt",
    )(q, k, g, v)


def _gla_sample_kernel(sb, t_new, q_ref, k_ref, g_ref, v_ref, s_in_ref, o_ref, s_out_ref):
    tpad = q_ref.shape[1]
    rowi = lax.broadcasted_iota(jnp.int32, (tpad, GLA_DK_H), 0)
    rowc = lax.broadcasted_iota(jnp.int32, (tpad, 1), 0)
    for i in range(sb):
        for h in range(GLA_HEADS):
            kcols = slice(h * GLA_DK_H, (h + 1) * GLA_DK_H)
            vcols = slice(h * GLA_DV_H, (h + 1) * GLA_DV_H)
            g = g_ref[i, :, kcols]
            q = q_ref[i, :, kcols]
            k = k_ref[i, :, kcols]
            v = v_ref[i, :, vcols]
            big_g = jnp.zeros((tpad, GLA_DK_H), F32)
            for u in range(t_new):
                big_g = big_g + jnp.where(rowi >= u, g[u:u + 1, :], 0.0)
            state = s_in_ref[0, i, h]
            o = _dot(q * jnp.exp(big_g), state)
            for s in range(t_new):
                w = jnp.exp(jnp.where(rowi >= s, big_g - big_g[s:s + 1, :], 0.0))
                a = jnp.sum(q * k[s:s + 1, :] * w, axis=-1, keepdims=True)
                o = o + jnp.where(rowc >= s, a, 0.0) * v[s:s + 1, :]
            o_ref[i, :, vcols] = o
            g_last = big_g[t_new - 1:t_new, :]
            k_dec = k * jnp.exp(g_last - big_g)
            g_last_col = jnp.broadcast_to(g_last, (tpad, GLA_DK_H)).T[:, 0:1]
            s_out_ref[0, i, h] = state * jnp.exp(g_last_col) + lax.dot_general(k_dec, v, TN, preferred_element_type=F32)


def _gla_sample(layer, t_new, q, k, g, v, state_all, sb):
    n_seq, tpad, _ = q.shape
    sds = jax.ShapeDtypeStruct
    kspec = pl.BlockSpec((sb, tpad, GLA_DK), lambda b: (b, 0, 0))
    vspec = pl.BlockSpec((sb, tpad, GLA_DV), lambda b: (b, 0, 0))
    sspec_in = pl.BlockSpec((1, sb, GLA_HEADS, GLA_DK_H, GLA_DV_H), lambda b: (layer, b, 0, 0, 0))
    sspec_out = pl.BlockSpec((1, sb, GLA_HEADS, GLA_DK_H, GLA_DV_H), lambda b: (0, b, 0, 0, 0))
    o, s_new = pl.pallas_call(
        functools.partial(_gla_sample_kernel, sb, t_new),
        grid=(n_seq // sb,),
        in_specs=[kspec, kspec, kspec, vspec, sspec_in],
        out_specs=[vspec, sspec_out],
        out_shape=[sds((n_seq, tpad, GLA_DV), F32), sds((1, n_seq, GLA_HEADS, GLA_DK_H, GLA_DV_H), F32)],
        compiler_params=_cparams(("parallel",)),
        name="gla_sample",
    )(q, k, g, v, state_all)
    return o, s_new[0]


def kernel(x_prompt, x_sample, cache_k, cache_v, cache_logf, state_gla, page_table, meta, norm_g, fox_w_in, fox_b_f, fox_w_out, gla_w_in, gla_w_gate2, gla_b_gate, gla_norm_g, gla_w_out, ffn_w_up, ffn_w_down):
    n_b, seq, _ = x_prompt.shape
    n_seq, t_new, _ = x_sample.shape
    depth = norm_g.shape[0]
    n_valid = N_META_TOK + seq
    length = -(-n_valid // LANES) * LANES
    tm_p = length // 4
    n_s_rows = n_seq * t_new
    dt = x_prompt.dtype

    xp = jnp.concatenate([jnp.broadcast_to(meta.astype(dt)[None], (n_b, N_META_TOK, D_MODEL)), x_prompt,
                          jnp.zeros((n_b, length - n_valid, D_MODEL), dt)], axis=1)
    xs = x_sample.reshape(1, n_s_rows, D_MODEL)
    tpad = 8

    def pad_rows(a):
        return jnp.pad(a.reshape(n_seq, t_new, a.shape[-1]), ((0, 0), (0, tpad - t_new), (0, 0)))

    n_pool = cache_k.shape[1]
    cache_kt = jnp.transpose(cache_k, (0, 1, 3, 4, 2)).reshape(cache_k.shape[0], n_pool, D_MODEL, PAGE)
    cache_vt = jnp.transpose(cache_v, (0, 1, 3, 4, 2)).reshape(cache_v.shape[0], n_pool, D_MODEL, PAGE)
    cache_lft = jnp.swapaxes(cache_logf, 2, 3)

    n_fox = (depth + 1) // 2
    kv_p = [jnp.zeros((n_fox, n_b, n_valid, D_MODEL), F32) for _ in range(2)]
    kv_s = [jnp.zeros((n_fox, 1, n_s_rows, D_MODEL), F32) for _ in range(2)]
    nf_p, ns_p, nf_s, ns_s = [], [], [], []
    for i in range(depth):
        g = norm_g[i].reshape(4, 1, D_MODEL)
        j = i // 2
        if i % 2 == 0:
            w_in = fox_w_in[j]
            wq = w_in[:, :D_MODEL].astype(BF16)
            wk = w_in[:, D_MODEL:2 * D_MODEL].astype(BF16)
            wv = w_in[:, 2 * D_MODEL:3 * D_MODEL].astype(BF16)
            wf = jnp.pad(w_in[:, 3 * D_MODEL:], ((0, 0), (0, LANES - FOX_HEADS))).astype(BF16)
            bf = fox_b_f[j].reshape(1, FOX_HEADS)
            w_out = fox_w_out[j].astype(BF16)
            q16, *kv_p, k16, v16, lf = _fox_proj(xp, g[0], wq, wk, wv, wf, bf, tm_p, FOX_SCALE * LOG2E, j, *kv_p)
            o16 = _fox_attn(q16, k16, v16, _fox_cumsum(lf))
            xp = _fox_out(xp, o16, w_out, g[1], tm_p, n_valid)
            nf_p.append(lf[:, :n_valid])
            q16s, *kv_s, _, _, lfs = _fox_proj(xs, g[0], wq, wk, wv, wf, bf, n_s_rows, FOX_SCALE, j, *kv_s)
            k32s, v32s = kv_s[0][j], kv_s[1][j]
            nt = _fox_new_bias(lfs[0], t_new)
            nt = jnp.transpose(nt.reshape(FOX_HEADS, n_seq, t_new), (1, 0, 2))
            nt = jnp.pad(nt, ((0, 0), (0, 0), (0, tpad - t_new)))
            os_ = _fox_sample(j, t_new, page_table, pad_rows(q16s.astype(F32)), pad_rows(k32s), pad_rows(v32s), nt,
                              cache_kt, cache_vt, cache_lft)
            xs = _fox_out(xs, os_[:, :t_new].reshape(1, n_s_rows, D_MODEL), w_out, g[1], n_s_rows, n_s_rows)
            nf_s.append(lfs.reshape(n_seq, t_new, FOX_HEADS))
        else:
            w_in = gla_w_in[j]
            o1, o2, o3, o4 = GLA_DK, 2 * GLA_DK, 2 * GLA_DK + GLA_DV, 2 * GLA_DK + 2 * GLA_DV
            wq = w_in[:, :o1].astype(BF16)
            wk = w_in[:, o1:o2].astype(BF16)
            wv = w_in[:, o2:o3].astype(BF16)
            wr = w_in[:, o3:o4].astype(BF16)
            wl = jnp.pad(w_in[:, o4:], ((0, 0), (0, LANES - GLA_RANK))).astype(BF16)
            wg2 = jnp.pad(gla_w_gate2[j], ((0, LANES - GLA_RANK), (0, 0))).astype(BF16)
            bg = gla_b_gate[j].reshape(1, GLA_DK)
            gn = gla_norm_g[j].reshape(1, GLA_DV)
            w_out = gla_w_out[j].astype(BF16)
            q, k, v, r, gate = _gla_proj(xp, g[0], wq, wk, wv, wr, wl, wg2, bg, tm_p)
            o, s_fin = _gla_prompt(q, k, gate, v, n_valid)
            xp = _gla_out(xp, o, r, gn, w_out, g[1], tm_p, n_valid)
            ns_p.append(s_fin)
            q, k, v, r, gate = _gla_proj(xs, g[0], wq, wk, wv, wr, wl, wg2, bg, n_s_rows)
            o, s_new = _gla_sample(j, t_new, pad_rows(q), pad_rows(k), pad_rows(gate), pad_rows(v.astype(F32)),
                                   state_gla, 4)
            xs = _gla_out(xs, o[:, :t_new].reshape(1, n_s_rows, GLA_DV), r, gn, w_out, g[1], n_s_rows, n_s_rows)
            ns_s.append(s_new)
        wup = ffn_w_up[i].astype(BF16)
        wdn = ffn_w_down[i].astype(BF16).reshape(N_FF_CHUNKS, FF_CHUNK, D_MODEL)
        xp = _ffn(xp, g[2], g[3], wup, wdn, 2 * tm_p)
        xs = _ffn(xs, g[2], g[3], wup, wdn, n_s_rows)

    y_prompt = xp[:, N_META_TOK:n_valid]
    y_sample = xs.reshape(n_seq, t_new, D_MODEL)
    heads = lambda a, rows: a.reshape(n_fox, -1, rows, FOX_HEADS, FOX_HEAD_DIM)
    return (y_prompt, y_sample,
            heads(kv_p[0], n_valid), heads(kv_p[1], n_valid), jnp.stack(nf_p), jnp.stack(ns_p),
            heads(kv_s[0], t_new), heads(kv_s[1], t_new), jnp.stack(nf_s), jnp.stack(ns_s))
```

```python
import functools

import jax
import jax.numpy as jnp
from jax import lax
from jax.experimental import pallas as pl
from jax.experimental.pallas import tpu as pltpu

D_MODEL = 1024
N_META_TOK = 16
FOX_HEADS = 16
FOX_HEAD_DIM = 64
FOX_SCALE = FOX_HEAD_DIM ** -0.5
GLA_HEADS = 4
GLA_DK = 512
GLA_DV = 1024
GLA_DK_H = 128
GLA_DV_H = 256
GLA_SCALE = GLA_DK_H ** -0.5
GLA_RANK = 16
GLA_TAU = 16.0
D_FF = 2816
RMS_EPS = 1e-6
NEG = -1e30
PAGE = 128

LANES = 128
FF_CHUNK = 256
N_FF_CHUNKS = D_FF // FF_CHUNK
GLA_CHUNK = 128
GLA_SUB = 16
EXP_CLAMP = 80.0
VMEM_LIMIT = 56 * 1024 * 1024

F32 = jnp.float32
BF16 = jnp.bfloat16
NT = (((1,), (1,)), ((), ()))
TN = (((0,), (0,)), ((), ()))


def _cparams(sem):
    return pltpu.CompilerParams(dimension_semantics=sem, vmem_limit_bytes=VMEM_LIMIT)


def _rms(x, g):
    return x * lax.rsqrt(jnp.mean(x * x, axis=-1, keepdims=True) + RMS_EPS) * g


def _dot(a, b):
    return jnp.dot(a, b, preferred_element_type=F32)


def _select_dot(sel, x, dims=None, sel_first=True):
    hi = x.astype(BF16)
    r1 = x - hi.astype(F32)
    mid = r1.astype(BF16)
    lo = (r1 - mid.astype(F32)).astype(BF16)
    sel = sel.astype(BF16)
    out = None
    for piece in (hi, mid, lo):
        a, b = (sel, piece) if sel_first else (piece, sel)
        d = _dot(a, b) if dims is None else lax.dot_general(a, b, dims, preferred_element_type=F32)
        out = d if out is None else out + d
    return out


def _log_sigmoid(z):
    return jnp.minimum(z, 0.0) - jnp.log1p(jnp.exp(-jnp.abs(z)))


def _const_spec(shape):
    n = len(shape)
    return pl.BlockSpec(shape, lambda *_: (0,) * n, pipeline_mode=pl.Buffered(1))


def _row_spec(tm, width):
    return pl.BlockSpec((1, tm, width), lambda b, i: (b, i, 0))


def _fox_proj_kernel(q_scale, x_ref, g_ref, wq_ref, wk_ref, wv_ref, wf_ref, bf_ref, k_all_ref, v_all_ref,
                     q16_ref, k_ref, v_ref, k16_ref, v16_ref, lf_ref):
    del k_all_ref, v_all_ref
    xn = _rms(x_ref[0], g_ref[...]).astype(BF16)
    q16_ref[0] = (_dot(xn, wq_ref[...]) * q_scale).astype(BF16)
    k = _dot(xn, wk_ref[...])
    k_ref[0] = k
    k16_ref[0] = k.astype(BF16)
    v = _dot(xn, wv_ref[...])
    v_ref[0] = v
    v16_ref[0] = v.astype(BF16)
    z = _dot(xn, wf_ref[...])[:, :FOX_HEADS] + bf_ref[...]
    lf_ref[0] = _log_sigmoid(z)


def _fox_proj(x, g, wq, wk, wv, wf, bf, tm, q_scale, layer, k_all, v_all):
    nb, length, _ = x.shape
    sds = jax.ShapeDtypeStruct
    slab_spec = pl.BlockSpec((None, 1, tm, D_MODEL), lambda b, i: (layer, b, i, 0))
    slab_shape = sds(k_all.shape, F32)
    any_spec = pl.BlockSpec(memory_space=pl.ANY)
    n_in = 7
    return pl.pallas_call(
        functools.partial(_fox_proj_kernel, q_scale),
        grid=(nb, length // tm),
        in_specs=[_row_spec(tm, D_MODEL), _const_spec((1, D_MODEL)),
                  _const_spec((D_MODEL, D_MODEL)), _const_spec((D_MODEL, D_MODEL)),
                  _const_spec((D_MODEL, D_MODEL)), _const_spec((D_MODEL, LANES)),
                  _const_spec((1, FOX_HEADS)), any_spec, any_spec],
        out_specs=[_row_spec(tm, D_MODEL), slab_spec, slab_spec, _row_spec(tm, D_MODEL), _row_spec(tm, D_MODEL),
                   _row_spec(tm, FOX_HEADS)],
        out_shape=[sds((nb, length, D_MODEL), BF16), slab_shape, slab_shape, sds((nb, length, D_MODEL), BF16),
                   sds((nb, length, D_MODEL), BF16), sds((nb, length, FOX_HEADS), F32)],
        input_output_aliases={n_in: 1, n_in + 1: 2},
        compiler_params=_cparams(("parallel", "parallel")),
        name="fox_proj",
    )(x, g, wq, wk, wv, wf, bf, k_all, v_all)


N_BIAS = 3
LOG2E = 1.4426950408889634


def _fox_cumsum_kernel(nblk, lf_ref, c3_ref):
    r = lax.broadcasted_iota(jnp.int32, (LANES, LANES), 0)
    c = lax.broadcasted_iota(jnp.int32, (LANES, LANES), 1)
    tri = (c <= r).astype(F32)
    hrow = lax.broadcasted_iota(jnp.int32, (FOX_HEADS, LANES), 0)
    lane = lax.broadcasted_iota(jnp.int32, (FOX_HEADS, LANES), 1)
    carry = jnp.zeros((1, FOX_HEADS), F32)
    for b in range(nblk):
        rows = slice(b * LANES, (b + 1) * LANES)
        cb = _select_dot(tri, lf_ref[0, rows, :]) + carry
        carry = cb[LANES - 1:LANES, :]
        rest = cb * LOG2E
        placed = None
        for j in range(N_BIAS):
            piece = rest.astype(BF16)
            rest = rest - piece.astype(F32)
            d = _dot(piece, (lane == hrow + FOX_HEADS * j).astype(BF16))
            placed = d if placed is None else placed + d
        c3_ref[0, rows, :] = placed.astype(BF16)


def _fox_cumsum(lf):
    nb, length, _ = lf.shape
    return pl.pallas_call(
        functools.partial(_fox_cumsum_kernel, length // LANES),
        grid=(nb,),
        in_specs=[pl.BlockSpec((1, length, FOX_HEADS), lambda b: (b, 0, 0))],
        out_specs=pl.BlockSpec((1, length, LANES), lambda b: (b, 0, 0)),
        out_shape=jax.ShapeDtypeStruct((nb, length, LANES), BF16),
        compiler_params=_cparams(("parallel",)),
        name="fox_cumsum",
    )(lf)


ATT_TQ = 256


def _fox_attn_kernel(q_ref, k_ref, v_ref, c_ref, o_ref, qa_ref, ka_ref, vt_ref, s_ref):
    hp = pl.program_id(1)
    length = q_ref.shape[1]
    half = FOX_HEAD_DIM

    prow = lax.broadcasted_iota(jnp.int32, (LANES, 2 * LANES), 0)
    col = lax.broadcasted_iota(jnp.int32, (LANES, 2 * LANES), 1)
    piece = prow // FOX_HEADS
    hrow = prow % FOX_HEADS
    is_k = col >= LANES
    lane_in = col - jnp.where(is_k, LANES, 0)
    start = jnp.where(hrow == 2 * hp, half, jnp.where(hrow == 2 * hp + 1, 0, -LANES)) + jnp.where(is_k, N_BIAS, 0)
    sel = jnp.where((piece < N_BIAS) & (lane_in == start + piece), jnp.where(is_k, -1.0, 1.0), 0.0)
    ext = _dot(c_ref[0], sel.astype(BF16))
    lane1 = lax.broadcasted_iota(jnp.int32, (1, LANES), 1)
    within = lane1 % half
    ones_q = ((within >= N_BIAS) & (within < 2 * N_BIAS)).astype(F32)
    ones_k = (within < N_BIAS).astype(F32)
    q_ext = (ext[:, :LANES] + ones_q).astype(BF16)
    k_ext = (ext[:, LANES:] + ones_k).astype(BF16)
    first = lane1 < half
    q2 = q_ref[0]
    k2 = k_ref[0]
    qa_ref[0] = jnp.where(first, q2, q_ext)
    qa_ref[1] = jnp.where(first, q_ext, q2)
    ka_ref[0] = jnp.where(first, k2, k_ext)
    ka_ref[1] = jnp.where(first, k_ext, k2)
    for b in range(length // LANES):
        rows = slice(b * LANES, (b + 1) * LANES)
        vt_ref[:, rows] = v_ref[0, rows, :].astype(F32).T.astype(BF16)

    def block(r0, tq, n_chunks):
        kk = lax.broadcasted_iota(jnp.int32, (tq, tq), 0)
        qq = lax.broadcasted_iota(jnp.int32, (tq, tq), 1)
        visible = kk <= qq
        qa = [qa_ref[i, pl.ds(r0, tq), :] for i in range(2)]
        vt_rows = [slice(i * half, (i + 1) * half) for i in range(2)]

        def sweep(fn, carry):
            n4 = n_chunks // 4
            n2 = (n_chunks - 4 * n4) // 2
            carry = lax.fori_loop(0, n4, lambda j4, c: fn(4 * j4, 4, c), carry)
            carry = lax.fori_loop(0, n2, lambda j2, c: fn(4 * n4 + 2 * j2, 2, c), carry)
            return lax.fori_loop(4 * n4 + 2 * n2, n_chunks, lambda j, c: fn(j, 1, c), carry)

        def scores(j, n, ms):
            koff = pl.multiple_of(j * ATT_TQ, ATT_TQ)
            out = []
            for i in range(2):
                s = lax.dot_general(ka_ref[i, pl.ds(koff, n * ATT_TQ), :], qa[i], NT, preferred_element_type=F32)
                s_ref[i, pl.ds(koff, n * ATT_TQ), 0:tq] = s
                out.append(jnp.maximum(ms[i], jnp.max(s, axis=0, keepdims=True)))
            return tuple(out)

        sd = [jnp.where(visible, lax.dot_general(ka_ref[i, pl.ds(r0, tq), :], qa[i], NT, preferred_element_type=F32),
                        NEG) for i in range(2)]
        ms = sweep(scores, tuple(jnp.max(sd[i], axis=0, keepdims=True) for i in range(2)))
        init = []
        for i in range(2):
            pd = jnp.exp2(sd[i] - ms[i])
            init += [jnp.sum(pd, axis=0, keepdims=True), _dot(vt_ref[vt_rows[i], pl.ds(r0, tq)], pd.astype(BF16))]

        def weighted(j, n, carry):
            out = list(carry)
            for u in range(n):
                koff = pl.multiple_of((j + u) * ATT_TQ, ATT_TQ)
                for i in range(2):
                    p = jnp.exp2(s_ref[i, pl.ds(koff, ATT_TQ), 0:tq] - ms[i])
                    out[2 * i] = out[2 * i] + jnp.sum(p, axis=0, keepdims=True)
                    out[2 * i + 1] = out[2 * i + 1] + _dot(vt_ref[vt_rows[i], pl.ds(koff, ATT_TQ)], p.astype(BF16))
            return tuple(out)

        la, acca, lb, accb = sweep(weighted, tuple(init))
        o_ref[0, pl.ds(r0, tq), :] = jnp.concatenate([acca / la, accb / lb], axis=0).T.astype(BF16)

    n_blocks = length // ATT_TQ

    def body(qb, _):
        block(pl.multiple_of(qb * ATT_TQ, ATT_TQ), ATT_TQ, qb)
        return 0

    lax.fori_loop(0, n_blocks, body, 0)
    tail = length - n_blocks * ATT_TQ
    if tail:
        block(n_blocks * ATT_TQ, tail, n_blocks)


def _fox_attn(q16, k16, v16, c3):
    nb, length, _ = q16.shape
    n_hp = FOX_HEADS // 2
    col_spec = pl.BlockSpec((1, length, LANES), lambda b, h: (b, 0, h))
    return pl.pallas_call(
        _fox_attn_kernel,
        grid=(nb, n_hp),
        in_specs=[col_spec, col_spec, col_spec,
                  pl.BlockSpec((1, length, LANES), lambda b, h: (b, 0, 0))],
        out_specs=col_spec,
        out_shape=jax.ShapeDtypeStruct((nb, length, D_MODEL), BF16),
        scratch_shapes=[pltpu.VMEM((2, length, LANES), BF16), pltpu.VMEM((2, length, LANES), BF16),
                        pltpu.VMEM((LANES, length), BF16),
                        pltpu.VMEM((2, length // ATT_TQ * ATT_TQ, ATT_TQ), F32)],
        compiler_params=_cparams(("parallel", "parallel")),
        name="fox_attn_prompt",
    )(q16, k16, v16, c3)


def _fox_new_bias_kernel(t_new, lf_ref, nt_ref):
    n = lf_ref.shape[0]
    r = lax.broadcasted_iota(jnp.int32, (n, n), 0)
    c = lax.broadcasted_iota(jnp.int32, (n, n), 1)
    same_seq = (r // t_new) == (c // t_new)
    bd = (same_seq & (c <= r)).astype(F32)
    cum = _select_dot(bd, lf_ref[...])
    er = lax.broadcasted_iota(jnp.int32, (FOX_HEADS, FOX_HEADS), 0)
    ec = lax.broadcasted_iota(jnp.int32, (FOX_HEADS, FOX_HEADS), 1)
    eye = (er == ec).astype(F32)
    nt_ref[...] = _select_dot(eye, cum, NT)


def _fox_new_bias(lf, t_new):
    n = lf.shape[0]
    return pl.pallas_call(
        functools.partial(_fox_new_bias_kernel, t_new),
        out_shape=jax.ShapeDtypeStruct((FOX_HEADS, n), F32),
        compiler_params=pltpu.CompilerParams(vmem_limit_bytes=VMEM_LIMIT),
        name="fox_new_bias",
    )(lf)


def _fox_sample_kernel(layer, t_new, n_pages, pt_ref, q_ref, kn_ref, vn_ref, nt_ref, ck_hbm, cv_hbm, clf_hbm,
                       o_ref, kbuf, vbuf, lfbuf, sems):
    b = pl.program_id(0)
    n_seq = pl.num_programs(0)
    rows = t_new * FOX_HEADS

    def page_copies(seq, slot):
        copies = []
        for p in range(n_pages):
            page = pt_ref[seq * n_pages + p]
            copies.append(pltpu.make_async_copy(ck_hbm.at[layer, page], kbuf.at[slot, :, pl.ds(p * PAGE, PAGE)],
                                                sems.at[slot, 0]))
            copies.append(pltpu.make_async_copy(cv_hbm.at[layer, page], vbuf.at[slot, :, pl.ds(p * PAGE, PAGE)],
                                                sems.at[slot, 1]))
            copies.append(pltpu.make_async_copy(clf_hbm.at[layer, page],
                                                lfbuf.at[slot, pl.ds(p * FOX_HEADS, FOX_HEADS)], sems.at[slot, 2]))
        return copies

    slot = b % 2

    @pl.when(b == 0)
    def _():
        for cp in page_copies(0, 0):
            cp.start()

    @pl.when(b + 1 < n_seq)
    def _():
        for cp in page_copies(b + 1, 1 - slot):
            cp.start()

    row = lax.broadcasted_iota(jnp.int32, (rows, D_MODEL), 0)
    lane = lax.broadcasted_iota(jnp.int32, (rows, D_MODEL), 1)
    own_head = (row % FOX_HEADS) == (lane // FOX_HEAD_DIM)
    q = q_ref[0]
    qrep = jnp.concatenate([jnp.broadcast_to(q[t:t + 1, :], (FOX_HEADS, D_MODEL)) for t in range(t_new)], axis=0)
    qexp = jnp.where(own_head, qrep, 0.0).astype(BF16)
    qf = qexp.astype(F32)
    kn = kn_ref[0].astype(BF16).astype(F32)
    vn = vn_ref[0]
    nt = nt_ref[0]
    row_t = lax.broadcasted_iota(jnp.int32, (rows, 1), 0) // FOX_HEADS
    cols = []
    for t2 in range(t_new):
        sc = jnp.sum(qf * kn[t2:t2 + 1, :], axis=-1, keepdims=True)
        sc = sc - jnp.concatenate([nt[:, t2:t2 + 1]] * t_new, axis=0)
        cols.append(jnp.where(row_t >= t2, sc, NEG))
    m_new = functools.reduce(jnp.maximum, cols)

    for cp in page_copies(b, slot):
        cp.wait()

    lft = lfbuf[slot]
    j = lax.broadcasted_iota(jnp.int32, (PAGE, PAGE), 0)
    key = lax.broadcasted_iota(jnp.int32, (PAGE, PAGE), 1)
    within = _select_dot((j > key).astype(F32), lft, sel_first=False)
    total = jnp.broadcast_to(within[:, 0:1] + lft[:, 0:1], (n_pages * FOX_HEADS, PAGE))
    pr = lax.broadcasted_iota(jnp.int32, (n_pages * FOX_HEADS, n_pages * FOX_HEADS), 0)
    pc = lax.broadcasted_iota(jnp.int32, (n_pages * FOX_HEADS, n_pages * FOX_HEADS), 1)
    later_page = ((pr % FOX_HEADS) == (pc % FOX_HEADS)) & (pc // FOX_HEADS > pr // FOX_HEADS)
    suf = within + _select_dot(later_page.astype(F32), total)
    bias = jnp.concatenate(
        [jnp.concatenate([suf[p * FOX_HEADS:(p + 1) * FOX_HEADS, :]] * t_new, axis=0) for p in range(n_pages)], axis=1)

    s = _dot(qf, kbuf[slot]) + bias
    m = jnp.maximum(m_new, jnp.max(s, axis=-1, keepdims=True))
    p_past = jnp.exp(s - m)
    p_new = [jnp.exp(sc - m) for sc in cols]
    l = jnp.sum(p_past, axis=-1, keepdims=True) + functools.reduce(jnp.add, p_new)
    acc = _dot(vbuf[slot], p_past.T).T
    acc = acc + functools.reduce(jnp.add, [p_new[t2] * vn[t2:t2 + 1, :] for t2 in range(t_new)])
    o = jnp.where(own_head, acc / l, 0.0)
    o_ref[0, 0:t_new, :] = o.reshape(t_new, FOX_HEADS, D_MODEL).sum(axis=1)
    o_ref[0, t_new:, :] = jnp.zeros((o_ref.shape[1] - t_new, D_MODEL), F32)


def _fox_sample(layer, t_new, page_table, q_new, k_new, v_new, nt, cache_kt, cache_vt, cache_lft):
    n_seq, tpad, _ = q_new.shape
    n_pages = page_table.shape[1]
    any_spec = pl.BlockSpec(memory_space=pl.ANY)
    grid_spec = pltpu.PrefetchScalarGridSpec(
        num_scalar_prefetch=1,
        grid=(n_seq,),
        in_specs=[pl.BlockSpec((1, tpad, D_MODEL), lambda b, pt: (b, 0, 0)),
                  pl.BlockSpec((1, tpad, D_MODEL), lambda b, pt: (b, 0, 0)),
                  pl.BlockSpec((1, tpad, D_MODEL), lambda b, pt: (b, 0, 0)),
                  pl.BlockSpec((1, FOX_HEADS, tpad), lambda b, pt: (b, 0, 0)),
                  any_spec, any_spec, any_spec],
        out_specs=pl.BlockSpec((1, tpad, D_MODEL), lambda b, pt: (b, 0, 0)),
        scratch_shapes=[pltpu.VMEM((2, D_MODEL, n_pages * PAGE), F32), pltpu.VMEM((2, D_MODEL, n_pages * PAGE), F32),
                        pltpu.VMEM((2, n_pages * FOX_HEADS, PAGE), F32), pltpu.SemaphoreType.DMA((2, 3))])
    return pl.pallas_call(
        functools.partial(_fox_sample_kernel, layer, t_new, n_pages),
        grid_spec=grid_spec,
        out_shape=jax.ShapeDtypeStruct((n_seq, tpad, D_MODEL), F32),
        compiler_params=_cparams(("arbitrary",)),
        name="fox_attn_sample",
    )(page_table.reshape(-1), q_new, k_new, v_new, nt, cache_kt, cache_vt, cache_lft)


def _valid_rows(tm, n_valid):
    row = pl.program_id(1) * tm + lax.broadcasted_iota(jnp.int32, (tm, 1), 0)
    return row < n_valid


def _fox_out_kernel(tm, n_valid, x_ref, o_ref, w_ref, g_ref, y_ref):
    m = _dot(o_ref[0].astype(BF16), w_ref[...])
    y_ref[0] = x_ref[0] + jnp.where(_valid_rows(tm, n_valid), _rms(m, g_ref[...]), 0.0)


def _fox_out(x, o, w, g, tm, n_valid):
    nb, length, _ = x.shape
    return pl.pallas_call(
        functools.partial(_fox_out_kernel, tm, n_valid),
        grid=(nb, length // tm),
        in_specs=[_row_spec(tm, D_MODEL), _row_spec(tm, D_MODEL),
                  _const_spec((D_MODEL, D_MODEL)), _const_spec((1, D_MODEL))],
        out_specs=_row_spec(tm, D_MODEL),
        out_shape=jax.ShapeDtypeStruct(x.shape, F32),
        compiler_params=_cparams(("parallel", "parallel")),
        name="fox_out",
    )(x, o, w, g)


def _gla_out_kernel(tm, n_valid, x_ref, o_ref, r_ref, gn_ref, w_ref, g_ref, y_ref):
    o = o_ref[0]
    r = r_ref[0]
    gn = gn_ref[...]
    parts = []
    for h in range(GLA_HEADS):
        cols = slice(h * GLA_DV_H, (h + 1) * GLA_DV_H)
        on = _rms(o[:, cols], gn[:, cols])
        rh = r[:, cols]
        parts.append((on * (rh * jax.nn.sigmoid(rh))).astype(BF16))
    m = _dot(jnp.concatenate(parts, axis=1), w_ref[...])
    y_ref[0] = x_ref[0] + jnp.where(_valid_rows(tm, n_valid), _rms(m, g_ref[...]), 0.0)


def _gla_out(x, o, r, gn, w, g, tm, n_valid):
    nb, length, _ = x.shape
    return pl.pallas_call(
        functools.partial(_gla_out_kernel, tm, n_valid),
        grid=(nb, length // tm),
        in_specs=[_row_spec(tm, D_MODEL), _row_spec(tm, GLA_DV), _row_spec(tm, GLA_DV),
                  _const_spec((1, GLA_DV)), _const_spec((GLA_DV, D_MODEL)), _const_spec((1, D_MODEL))],
        out_specs=_row_spec(tm, D_MODEL),
        out_shape=jax.ShapeDtypeStruct(x.shape, F32),
        compiler_params=_cparams(("parallel", "parallel")),
        name="gla_out",
    )(x, o, r, gn, w, g)


def _ffn_residual(y_ref, g2_ref, g3_ref, wup_ref, wdn_ref, xn_ref, acc_ref):
    xn_ref[...] = _rms(y_ref[0], g2_ref[...]).astype(BF16)
    acc_ref[...] = jnp.zeros(acc_ref.shape, F32)

    def chunk(c, _):
        xn = xn_ref[...]
        col = pl.multiple_of(c * FF_CHUNK, FF_CHUNK)
        h1 = _dot(xn, wup_ref[:, pl.ds(col, FF_CHUNK)])
        h2 = _dot(xn, wup_ref[:, pl.ds(D_FF + col, FF_CHUNK)])
        a = (h1 * jax.nn.sigmoid(h1) * h2).astype(BF16)
        acc_ref[...] += _dot(a, wdn_ref[c])
        return 0

    lax.fori_loop(0, N_FF_CHUNKS, chunk, 0)
    y_ref[0] = y_ref[0] + _rms(acc_ref[...], g3_ref[...])


def _ffn_kernel(x_ref, g2_ref, g3_ref, wup_ref, wdn_ref, y_ref, xn_ref, acc_ref):
    y_ref[0] = x_ref[0]
    _ffn_residual(y_ref, g2_ref, g3_ref, wup_ref, wdn_ref, xn_ref, acc_ref)


def _fox_out_ffn_kernel(tm, n_valid, x_ref, o_ref, wout_ref, g1_ref, g2_ref, g3_ref, wup_ref, wdn_ref,
                        y_ref, xn_ref, acc_ref):
    m = _dot(o_ref[0], wout_ref[...])
    y_ref[0] = x_ref[0] + jnp.where(_valid_rows(tm, n_valid), _rms(m, g1_ref[...]), 0.0)
    _ffn_residual(y_ref, g2_ref, g3_ref, wup_ref, wdn_ref, xn_ref, acc_ref)


def _ffn(x, g2, g3, wup, wdn, tm, mix=None):
    nb, length, _ = x.shape
    ffn_specs = [_const_spec((1, D_MODEL)), _const_spec((1, D_MODEL)),
                 _const_spec((D_MODEL, 2 * D_FF)), _const_spec((N_FF_CHUNKS, FF_CHUNK, D_MODEL))]
    if mix is None:
        body, in_specs, args = _ffn_kernel, [_row_spec(tm, D_MODEL)] + ffn_specs, (x, g2, g3, wup, wdn)
    else:
        o16, w_out, g1, n_valid = mix
        body = functools.partial(_fox_out_ffn_kernel, tm, n_valid)
        in_specs = [_row_spec(tm, D_MODEL), _row_spec(tm, D_MODEL), _const_spec((D_MODEL, D_MODEL)),
                    _const_spec((1, D_MODEL))] + ffn_specs
        args = (x, o16, w_out, g1, g2, g3, wup, wdn)
    return pl.pallas_call(
        body,
        grid=(nb, length // tm),
        in_specs=in_specs,
        out_specs=_row_spec(tm, D_MODEL),
        out_shape=jax.ShapeDtypeStruct(x.shape, F32),
        scratch_shapes=[pltpu.VMEM((tm, D_MODEL), BF16), pltpu.VMEM((tm, D_MODEL), F32)],
        compiler_params=_cparams(("parallel", "parallel")),
        name="ffn",
    )(*args)


def _gla_proj_kernel(x_ref, g_ref, wq_ref, wk_ref, wv_ref, wr_ref, wl_ref, wg2_ref, bg_ref,
                     q_ref, k_ref, v_ref, r_ref, gate_ref):
    xn = _rms(x_ref[0], g_ref[...]).astype(BF16)
    q_ref[0] = _dot(xn, wq_ref[...]) * GLA_SCALE
    k_ref[0] = _dot(xn, wk_ref[...])
    v_ref[0] = _dot(xn, wv_ref[...]).astype(BF16)
    r_ref[0] = _dot(xn, wr_ref[...])
    low = _dot(xn, wl_ref[...]).astype(BF16)
    z = _dot(low, wg2_ref[...]) + bg_ref[...]
    gate_ref[0] = _log_sigmoid(z) * (1.0 / GLA_TAU)


def _gla_proj(x, g, wq, wk, wv, wr, wl, wg2, bg, tm):
    nb, length, _ = x.shape
    sds = jax.ShapeDtypeStruct
    return pl.pallas_call(
        _gla_proj_kernel,
        grid=(nb, length // tm),
        in_specs=[_row_spec(tm, D_MODEL), _const_spec((1, D_MODEL)),
                  _const_spec((D_MODEL, GLA_DK)), _const_spec((D_MODEL, GLA_DK)),
                  _const_spec((D_MODEL, GLA_DV)), _const_spec((D_MODEL, GLA_DV)),
                  _const_spec((D_MODEL, LANES)), _const_spec((LANES, GLA_DK)), _const_spec((1, GLA_DK))],
        out_specs=[_row_spec(tm, GLA_DK), _row_spec(tm, GLA_DK), _row_spec(tm, GLA_DV),
                   _row_spec(tm, GLA_DV), _row_spec(tm, GLA_DK)],
        out_shape=[sds((nb, length, GLA_DK), F32), sds((nb, length, GLA_DK), F32),
                   sds((nb, length, GLA_DV), BF16), sds((nb, length, GLA_DV), F32),
                   sds((nb, length, GLA_DK), F32)],
        compiler_params=_cparams(("parallel", "parallel")),
        name="gla_proj",
    )(x, g, wq, wk, wv, wr, wl, wg2, bg)


GLA_HPS = 2


def _gla_prompt_kernel(n_chunks, n_valid, q_ref, k_ref, g_ref, v_ref, o_ref, s_out_ref, s_ref):
    s_ref[...] = jnp.zeros(s_ref.shape, F32)
    r = lax.broadcasted_iota(jnp.int32, (GLA_CHUNK, GLA_CHUNK), 0)
    c = lax.broadcasted_iota(jnp.int32, (GLA_CHUNK, GLA_CHUNK), 1)
    causal = c <= r
    tri = causal.astype(F32)
    n_sub = GLA_CHUNK // GLA_SUB

    def chunk(ci, _):
        off = pl.multiple_of(ci * GLA_CHUNK, GLA_CHUNK)
        rows = pl.ds(off, GLA_CHUNK)
        pos = off + lax.broadcasted_iota(jnp.int32, (GLA_CHUNK, 1), 0)
        for hh in range(GLA_HPS):
            kcols = slice(hh * GLA_DK_H, (hh + 1) * GLA_DK_H)
            vcols = slice(hh * GLA_DV_H, (hh + 1) * GLA_DV_H)
            g = jnp.where(pos < n_valid, g_ref[0, rows, kcols], 0.0)
            q = q_ref[0, rows, kcols]
            k = k_ref[0, rows, kcols]
            v = v_ref[0, rows, vcols]
            big_g = _select_dot(tri, g)
            g_last = big_g[GLA_CHUNK - 1:GLA_CHUNK, :]
            g_last_col = big_g.T[:, GLA_CHUNK - 1:GLA_CHUNK]
            state = s_ref[hh]
            o = _dot((q * jnp.exp(big_g)).astype(BF16), state.astype(BF16))
            a_parts = []
            for i in range(n_sub):
                lo, hi = i * GLA_SUB, (i + 1) * GLA_SUB
                ref = big_g[lo - 1:lo, :] if i > 0 else jnp.zeros((1, GLA_DK_H), F32)
                q_i = (q[lo:hi, :] * jnp.exp(big_g[lo:hi, :] - ref)).astype(BF16)
                e = jnp.where(r < hi, jnp.exp(jnp.minimum(ref - big_g, EXP_CLAMP)), 0.0)
                a_parts.append(lax.dot_general(q_i, (k * e).astype(BF16), NT, preferred_element_type=F32))
            a = jnp.where(causal, jnp.concatenate(a_parts, axis=0), 0.0)
            o_ref[0, rows, vcols] = o + _dot(a.astype(BF16), v)
            k_dec = (k * jnp.exp(g_last - big_g)).T.astype(BF16)
            s_ref[hh] = state * jnp.exp(g_last_col) + _dot(k_dec, v)
        return 0

    lax.fori_loop(0, n_chunks, chunk, 0)
    s_out_ref[0] = s_ref[...]


def _gla_prompt(q, k, g, v, n_valid):
    nb, length, _ = q.shape
    sds = jax.ShapeDtypeStruct
    kspec = pl.BlockSpec((1, length, GLA_HPS * GLA_DK_H), lambda b, h: (b, 0, h))
    vspec = pl.BlockSpec((1, length, GLA_HPS * GLA_DV_H), lambda b, h: (b, 0, h))
    return pl.pallas_call(
        functools.partial(_gla_prompt_kernel, length // GLA_CHUNK, n_valid),
        grid=(nb, GLA_HEADS // GLA_HPS),
        in_specs=[kspec, kspec, kspec, vspec],
        out_specs=[vspec, pl.BlockSpec((1, GLA_HPS, GLA_DK_H, GLA_DV_H), lambda b, h: (b, h, 0, 0))],
        out_shape=[sds((nb, length, GLA_DV), F32), sds((nb, GLA_HEADS, GLA_DK_H, GLA_DV_H), F32)],
        scratch_shapes=[pltpu.VMEM((GLA_HPS, GLA_DK_H, GLA_DV_H), F32)],
        compiler_params=_cparams(("parallel", "parallel")),
        name="gla_prompt",
    )(q, k, g, v)


def _gla_sample_kernel(sb, t_new, q_ref, k_ref, g_ref, v_ref, s_in_ref, o_ref, s_out_ref):
    tpad = q_ref.shape[1]
    rowi = lax.broadcasted_iota(jnp.int32, (tpad, GLA_DK_H), 0)
    rowc = lax.broadcasted_iota(jnp.int32, (tpad, 1), 0)
    for i in range(sb):
        for h in range(GLA_HEADS):
            kcols = slice(h * GLA_DK_H, (h + 1) * GLA_DK_H)
            vcols = slice(h * GLA_DV_H, (h + 1) * GLA_DV_H)
            g = g_ref[i, :, kcols]
            q = q_ref[i, :, kcols]
            k = k_ref[i, :, kcols]
            v = v_ref[i, :, vcols]
            big_g = jnp.zeros((tpad, GLA_DK_H), F32)
            for u in range(t_new):
                big_g = big_g + jnp.where(rowi >= u, g[u:u + 1, :], 0.0)
            state = s_in_ref[0, i, h]
            o = _dot(q * jnp.exp(big_g), state)
            for s in range(t_new):
                w = jnp.exp(jnp.where(rowi >= s, big_g - big_g[s:s + 1, :], 0.0))
                a = jnp.sum(q * k[s:s + 1, :] * w, axis=-1, keepdims=True)
                o = o + jnp.where(rowc >= s, a, 0.0) * v[s:s + 1, :]
            o_ref[i, :, vcols] = o
            g_last = big_g[t_new - 1:t_new, :]
            k_dec = k * jnp.exp(g_last - big_g)
            g_last_col = jnp.broadcast_to(g_last, (tpad, GLA_DK_H)).T[:, 0:1]
            s_out_ref[0, i, h] = state * jnp.exp(g_last_col) + lax.dot_general(k_dec, v, TN, preferred_element_type=F32)


def _gla_sample(layer, t_new, q, k, g, v, state_all, sb):
    n_seq, tpad, _ = q.shape
    sds = jax.ShapeDtypeStruct
    kspec = pl.BlockSpec((sb, tpad, GLA_DK), lambda b: (b, 0, 0))
    vspec = pl.BlockSpec((sb, tpad, GLA_DV), lambda b: (b, 0, 0))
    sspec_in = pl.BlockSpec((1, sb, GLA_HEADS, GLA_DK_H, GLA_DV_H), lambda b: (layer, b, 0, 0, 0))
    sspec_out = pl.BlockSpec((1, sb, GLA_HEADS, GLA_DK_H, GLA_DV_H), lambda b: (0, b, 0, 0, 0))
    o, s_new = pl.pallas_call(
        functools.partial(_gla_sample_kernel, sb, t_new),
        grid=(n_seq // sb,),
        in_specs=[kspec, kspec, kspec, vspec, sspec_in],
        out_specs=[vspec, sspec_out],
        out_shape=[sds((n_seq, tpad, GLA_DV), F32), sds((1, n_seq, GLA_HEADS, GLA_DK_H, GLA_DV_H), F32)],
        compiler_params=_cparams(("parallel",)),
        name="gla_sample",
    )(q, k, g, v, state_all)
    return o, s_new[0]


def kernel(x_prompt, x_sample, cache_k, cache_v, cache_logf, state_gla, page_table, meta, norm_g, fox_w_in, fox_b_f, fox_w_out, gla_w_in, gla_w_gate2, gla_b_gate, gla_norm_g, gla_w_out, ffn_w_up, ffn_w_down):
    n_b, seq, _ = x_prompt.shape
    n_seq, t_new, _ = x_sample.shape
    depth = norm_g.shape[0]
    n_valid = N_META_TOK + seq
    length = -(-n_valid // LANES) * LANES
    tm_p = length // 4
    tm_big = length // 2
    n_s_rows = n_seq * t_new
    dt = x_prompt.dtype

    xp = jnp.concatenate([jnp.broadcast_to(meta.astype(dt)[None], (n_b, N_META_TOK, D_MODEL)), x_prompt,
                          jnp.zeros((n_b, length - n_valid, D_MODEL), dt)], axis=1)
    xs = x_sample.reshape(1, n_s_rows, D_MODEL)
    tpad = 8

    def pad_rows(a):
        return jnp.pad(a.reshape(n_seq, t_new, a.shape[-1]), ((0, 0), (0, tpad - t_new), (0, 0)))

    n_pool = cache_k.shape[1]
    cache_kt = jnp.transpose(cache_k, (0, 1, 3, 4, 2)).reshape(cache_k.shape[0], n_pool, D_MODEL, PAGE)
    cache_vt = jnp.transpose(cache_v, (0, 1, 3, 4, 2)).reshape(cache_v.shape[0], n_pool, D_MODEL, PAGE)
    cache_lft = jnp.swapaxes(cache_logf, 2, 3)

    n_fox = (depth + 1) // 2
    kv_p = [jnp.zeros((n_fox, n_b, n_valid, D_MODEL), F32) for _ in range(2)]
    kv_s = [jnp.zeros((n_fox, 1, n_s_rows, D_MODEL), F32) for _ in range(2)]
    nf_p, ns_p, nf_s, ns_s = [], [], [], []
    for i in range(depth):
        g = norm_g[i].reshape(4, 1, D_MODEL)
        j = i // 2
        if i % 2 == 0:
            w_in = fox_w_in[j]
            wq = w_in[:, :D_MODEL].astype(BF16)
            wk = w_in[:, D_MODEL:2 * D_MODEL].astype(BF16)
            wv = w_in[:, 2 * D_MODEL:3 * D_MODEL].astype(BF16)
            wf = jnp.pad(w_in[:, 3 * D_MODEL:], ((0, 0), (0, LANES - FOX_HEADS))).astype(BF16)
            bf = fox_b_f[j].reshape(1, FOX_HEADS)
            w_out = fox_w_out[j].astype(BF16)
            q16, *kv_p, k16, v16, lf = _fox_proj(xp, g[0], wq, wk, wv, wf, bf, tm_big, FOX_SCALE * LOG2E, j, *kv_p)
            o16 = _fox_attn(q16, k16, v16, _fox_cumsum(lf))
            mix = (o16, w_out, g[1], n_valid)
            nf_p.append(lf[:, :n_valid])
            q16s, *kv_s, _, _, lfs = _fox_proj(xs, g[0], wq, wk, wv, wf, bf, n_s_rows, FOX_SCALE, j, *kv_s)
            k32s, v32s = kv_s[0][j], kv_s[1][j]
            nt = _fox_new_bias(lfs[0], t_new)
            nt = jnp.transpose(nt.reshape(FOX_HEADS, n_seq, t_new), (1, 0, 2))
            nt = jnp.pad(nt, ((0, 0), (0, 0), (0, tpad - t_new)))
            os_ = _fox_sample(j, t_new, page_table, pad_rows(q16s.astype(F32)), pad_rows(k32s), pad_rows(v32s), nt,
                              cache_kt, cache_vt, cache_lft)
            xs = _fox_out(xs, os_[:, :t_new].reshape(1, n_s_rows, D_MODEL), w_out, g[1], n_s_rows, n_s_rows)
            nf_s.append(lfs.reshape(n_seq, t_new, FOX_HEADS))
        else:
            w_in = gla_w_in[j]
            o1, o2, o3, o4 = GLA_DK, 2 * GLA_DK, 2 * GLA_DK + GLA_DV, 2 * GLA_DK + 2 * GLA_DV
            wq = w_in[:, :o1].astype(BF16)
            wk = w_in[:, o1:o2].astype(BF16)
            wv = w_in[:, o2:o3].astype(BF16)
            wr = w_in[:, o3:o4].astype(BF16)
            wl = jnp.pad(w_in[:, o4:], ((0, 0), (0, LANES - GLA_RANK))).astype(BF16)
            wg2 = jnp.pad(gla_w_gate2[j], ((0, LANES - GLA_RANK), (0, 0))).astype(BF16)
            bg = gla_b_gate[j].reshape(1, GLA_DK)
            gn = gla_norm_g[j].reshape(1, GLA_DV)
            w_out = gla_w_out[j].astype(BF16)
            q, k, v, r, gate = _gla_proj(xp, g[0], wq, wk, wv, wr, wl, wg2, bg, tm_big)
            mix = None
            o, s_fin = _gla_prompt(q, k, gate, v, n_valid)
            xp = _gla_out(xp, o, r, gn, w_out, g[1], tm_p, n_valid)
            ns_p.append(s_fin)
            q, k, v, r, gate = _gla_proj(xs, g[0], wq, wk, wv, wr, wl, wg2, bg, n_s_rows)
            o, s_new = _gla_sample(j, t_new, pad_rows(q), pad_rows(k), pad_rows(gate), pad_rows(v.astype(F32)),
                                   state_gla, 4)
            xs = _gla_out(xs, o[:, :t_new].reshape(1, n_s_rows, GLA_DV), r, gn, w_out, g[1], n_s_rows, n_s_rows)
            ns_s.append(s_new)
        wup = ffn_w_up[i].astype(BF16)
        wdn = ffn_w_down[i].astype(BF16).reshape(N_FF_CHUNKS, FF_CHUNK, D_MODEL)
        xp = _ffn(xp, g[2], g[3], wup, wdn, tm_big, mix)
        xs = _ffn(xs, g[2], g[3], wup, wdn, n_s_rows)

    y_prompt = xp[:, N_META_TOK:n_valid]
    y_sample = xs.reshape(n_seq, t_new, D_MODEL)
    heads = lambda a, rows: a.reshape(n_fox, -1, rows, FOX_HEADS, FOX_HEAD_DIM)
    return (y_prompt, y_sample,
            heads(kv_p[0], n_valid), heads(kv_p[1], n_valid), jnp.stack(nf_p), jnp.stack(ns_p),
            heads(kv_s[0], t_new), heads(kv_s[1], t_new), jnp.stack(nf_s), jnp.stack(ns_s))
```

```python
import functools

import jax
import jax.numpy as jnp
from jax import lax
from jax.experimental import pallas as pl
from jax.experimental.pallas import tpu as pltpu

D_MODEL = 1024
N_META_TOK = 16
FOX_HEADS = 16
FOX_HEAD_DIM = 64
FOX_SCALE = FOX_HEAD_DIM ** -0.5
GLA_HEADS = 4
GLA_DK = 512
GLA_DV = 1024
GLA_DK_H = 128
GLA_DV_H = 256
GLA_SCALE = GLA_DK_H ** -0.5
GLA_RANK = 16
GLA_TAU = 16.0
D_FF = 2816
RMS_EPS = 1e-6
NEG = -1e30
PAGE = 128

LANES = 128
FF_CHUNK = 256
N_FF_CHUNKS = D_FF // FF_CHUNK
GLA_CHUNK = 128
GLA_SUB = 16
EXP_CLAMP = 80.0
VMEM_LIMIT = 56 * 1024 * 1024

F32 = jnp.float32
BF16 = jnp.bfloat16
NT = (((1,), (1,)), ((), ()))
TN = (((0,), (0,)), ((), ()))


def _cparams(sem):
    return pltpu.CompilerParams(dimension_semantics=sem, vmem_limit_bytes=VMEM_LIMIT)


def _rms(x, g):
    return x * lax.rsqrt(jnp.mean(x * x, axis=-1, keepdims=True) + RMS_EPS) * g


def _dot(a, b):
    return jnp.dot(a, b, preferred_element_type=F32)


def _select_dot(sel, x, dims=None, sel_first=True):
    hi = x.astype(BF16)
    r1 = x - hi.astype(F32)
    mid = r1.astype(BF16)
    lo = (r1 - mid.astype(F32)).astype(BF16)
    sel = sel.astype(BF16)
    out = None
    for piece in (hi, mid, lo):
        a, b = (sel, piece) if sel_first else (piece, sel)
        d = _dot(a, b) if dims is None else lax.dot_general(a, b, dims, preferred_element_type=F32)
        out = d if out is None else out + d
    return out


def _log_sigmoid(z):
    return jnp.minimum(z, 0.0) - jnp.log1p(jnp.exp(-jnp.abs(z)))


def _const_spec(shape):
    n = len(shape)
    return pl.BlockSpec(shape, lambda *_: (0,) * n, pipeline_mode=pl.Buffered(1))


def _row_spec(tm, width):
    return pl.BlockSpec((1, tm, width), lambda b, i: (b, i, 0))


def _fox_proj_kernel(q_scale, x_ref, g_ref, wq_ref, wk_ref, wv_ref, wf_ref, bf_ref, k_all_ref, v_all_ref,
                     q16_ref, k_ref, v_ref, k16_ref, v16_ref, lf_ref):
    del k_all_ref, v_all_ref
    xn = _rms(x_ref[0], g_ref[...]).astype(BF16)
    q16_ref[0] = (_dot(xn, wq_ref[...]) * q_scale).astype(BF16)
    k = _dot(xn, wk_ref[...])
    k_ref[0] = k
    k16_ref[0] = k.astype(BF16)
    v = _dot(xn, wv_ref[...])
    v_ref[0] = v
    v16_ref[0] = v.astype(BF16)
    z = _dot(xn, wf_ref[...])[:, :FOX_HEADS] + bf_ref[...]
    lf_ref[0] = _log_sigmoid(z)


def _fox_proj(x, g, wq, wk, wv, wf, bf, tm, q_scale, layer, k_all, v_all):
    nb, length, _ = x.shape
    sds = jax.ShapeDtypeStruct
    slab_spec = pl.BlockSpec((None, 1, tm, D_MODEL), lambda b, i: (layer, b, i, 0))
    slab_shape = sds(k_all.shape, F32)
    any_spec = pl.BlockSpec(memory_space=pl.ANY)
    n_in = 7
    return pl.pallas_call(
        functools.partial(_fox_proj_kernel, q_scale),
        grid=(nb, length // tm),
        in_specs=[_row_spec(tm, D_MODEL), _const_spec((1, D_MODEL)),
                  _const_spec((D_MODEL, D_MODEL)), _const_spec((D_MODEL, D_MODEL)),
                  _const_spec((D_MODEL, D_MODEL)), _const_spec((D_MODEL, LANES)),
                  _const_spec((1, FOX_HEADS)), any_spec, any_spec],
        out_specs=[_row_spec(tm, D_MODEL), slab_spec, slab_spec, _row_spec(tm, D_MODEL), _row_spec(tm, D_MODEL),
                   _row_spec(tm, FOX_HEADS)],
        out_shape=[sds((nb, length, D_MODEL), BF16), slab_shape, slab_shape, sds((nb, length, D_MODEL), BF16),
                   sds((nb, length, D_MODEL), BF16), sds((nb, length, FOX_HEADS), F32)],
        input_output_aliases={n_in: 1, n_in + 1: 2},
        compiler_params=_cparams(("parallel", "parallel")),
        name="fox_proj",
    )(x, g, wq, wk, wv, wf, bf, k_all, v_all)


N_BIAS = 3
LOG2E = 1.4426950408889634


def _fox_cumsum_kernel(nblk, lf_ref, c3_ref):
    r = lax.broadcasted_iota(jnp.int32, (LANES, LANES), 0)
    c = lax.broadcasted_iota(jnp.int32, (LANES, LANES), 1)
    tri = (c <= r).astype(F32)
    hrow = lax.broadcasted_iota(jnp.int32, (FOX_HEADS, LANES), 0)
    lane = lax.broadcasted_iota(jnp.int32, (FOX_HEADS, LANES), 1)
    carry = jnp.zeros((1, FOX_HEADS), F32)
    for b in range(nblk):
        rows = slice(b * LANES, (b + 1) * LANES)
        cb = _select_dot(tri, lf_ref[0, rows, :]) + carry
        carry = cb[LANES - 1:LANES, :]
        rest = cb * LOG2E
        placed = None
        for j in range(N_BIAS):
            piece = rest.astype(BF16)
            rest = rest - piece.astype(F32)
            d = _dot(piece, (lane == hrow + FOX_HEADS * j).astype(BF16))
            placed = d if placed is None else placed + d
        c3_ref[0, rows, :] = placed.astype(BF16)


def _fox_cumsum(lf):
    nb, length, _ = lf.shape
    return pl.pallas_call(
        functools.partial(_fox_cumsum_kernel, length // LANES),
        grid=(nb,),
        in_specs=[pl.BlockSpec((1, length, FOX_HEADS), lambda b: (b, 0, 0))],
        out_specs=pl.BlockSpec((1, length, LANES), lambda b: (b, 0, 0)),
        out_shape=jax.ShapeDtypeStruct((nb, length, LANES), BF16),
        compiler_params=_cparams(("parallel",)),
        name="fox_cumsum",
    )(lf)


ATT_TQ = 256


def _fox_attn_kernel(q_ref, k_ref, v_ref, c_ref, o_ref, qa_ref, ka_ref, vt_ref, s_ref):
    hp = pl.program_id(1)
    length = q_ref.shape[1]
    half = FOX_HEAD_DIM

    prow = lax.broadcasted_iota(jnp.int32, (LANES, 2 * LANES), 0)
    col = lax.broadcasted_iota(jnp.int32, (LANES, 2 * LANES), 1)
    piece = prow // FOX_HEADS
    hrow = prow % FOX_HEADS
    is_k = col >= LANES
    lane_in = col - jnp.where(is_k, LANES, 0)
    start = jnp.where(hrow == 2 * hp, half, jnp.where(hrow == 2 * hp + 1, 0, -LANES)) + jnp.where(is_k, N_BIAS, 0)
    sel = jnp.where((piece < N_BIAS) & (lane_in == start + piece), jnp.where(is_k, -1.0, 1.0), 0.0)
    ext = _dot(c_ref[0], sel.astype(BF16))
    lane1 = lax.broadcasted_iota(jnp.int32, (1, LANES), 1)
    within = lane1 % half
    ones_q = ((within >= N_BIAS) & (within < 2 * N_BIAS)).astype(F32)
    ones_k = (within < N_BIAS).astype(F32)
    q_ext = (ext[:, :LANES] + ones_q).astype(BF16)
    k_ext = (ext[:, LANES:] + ones_k).astype(BF16)
    first = lane1 < half
    q2 = q_ref[0]
    k2 = k_ref[0]
    qa_ref[0] = jnp.where(first, q2, q_ext)
    qa_ref[1] = jnp.where(first, q_ext, q2)
    ka_ref[0] = jnp.where(first, k2, k_ext)
    ka_ref[1] = jnp.where(first, k_ext, k2)
    for b in range(length // LANES):
        rows = slice(b * LANES, (b + 1) * LANES)
        vt_ref[:, rows] = v_ref[0, rows, :].astype(F32).T.astype(BF16)

    def block(r0, tq, n_chunks):
        kk = lax.broadcasted_iota(jnp.int32, (tq, tq), 0)
        qq = lax.broadcasted_iota(jnp.int32, (tq, tq), 1)
        visible = kk <= qq
        qa = [qa_ref[i, pl.ds(r0, tq), :] for i in range(2)]
        vt_rows = [slice(i * half, (i + 1) * half) for i in range(2)]

        def sweep(fn, carry):
            n4 = n_chunks // 4
            n2 = (n_chunks - 4 * n4) // 2
            carry = lax.fori_loop(0, n4, lambda j4, c: fn(4 * j4, 4, c), carry)
            carry = lax.fori_loop(0, n2, lambda j2, c: fn(4 * n4 + 2 * j2, 2, c), carry)
            return lax.fori_loop(4 * n4 + 2 * n2, n_chunks, lambda j, c: fn(j, 1, c), carry)

        def scores(j, n, ms):
            koff = pl.multiple_of(j * ATT_TQ, ATT_TQ)
            out = []
            for i in range(2):
                s = lax.dot_general(ka_ref[i, pl.ds(koff, n * ATT_TQ), :], qa[i], NT, preferred_element_type=F32)
                s_ref[i, pl.ds(koff, n * ATT_TQ), 0:tq] = s
                out.append(jnp.maximum(ms[i], jnp.max(s, axis=0, keepdims=True)))
            return tuple(out)

        sd = [jnp.where(visible, lax.dot_general(ka_ref[i, pl.ds(r0, tq), :], qa[i], NT, preferred_element_type=F32),
                        NEG) for i in range(2)]
        ms = sweep(scores, tuple(jnp.max(sd[i], axis=0, keepdims=True) for i in range(2)))
        init = []
        for i in range(2):
            pd = jnp.exp2(sd[i] - ms[i])
            init += [jnp.sum(pd, axis=0, keepdims=True), _dot(vt_ref[vt_rows[i], pl.ds(r0, tq)], pd.astype(BF16))]

        def weighted(j, n, carry):
            out = list(carry)
            for u in range(n):
                koff = pl.multiple_of((j + u) * ATT_TQ, ATT_TQ)
                for i in range(2):
                    p = jnp.exp2(s_ref[i, pl.ds(koff, ATT_TQ), 0:tq] - ms[i])
                    out[2 * i] = out[2 * i] + jnp.sum(p, axis=0, keepdims=True)
                    out[2 * i + 1] = out[2 * i + 1] + _dot(vt_ref[vt_rows[i], pl.ds(koff, ATT_TQ)], p.astype(BF16))
            return tuple(out)

        la, acca, lb, accb = sweep(weighted, tuple(init))
        o_ref[0, pl.ds(r0, tq), :] = jnp.concatenate([acca / la, accb / lb], axis=0).T.astype(BF16)

    n_blocks = length // ATT_TQ

    def body(qb, _):
        block(pl.multiple_of(qb * ATT_TQ, ATT_TQ), ATT_TQ, qb)
        return 0

    lax.fori_loop(0, n_blocks, body, 0)
    tail = length - n_blocks * ATT_TQ
    if tail:
        block(n_blocks * ATT_TQ, tail, n_blocks)


def _fox_attn(q16, k16, v16, c3):
    nb, length, _ = q16.shape
    n_hp = FOX_HEADS // 2
    col_spec = pl.BlockSpec((1, length, LANES), lambda b, h: (b, 0, h))
    return pl.pallas_call(
        _fox_attn_kernel,
        grid=(nb, n_hp),
        in_specs=[col_spec, col_spec, col_spec,
                  pl.BlockSpec((1, length, LANES), lambda b, h: (b, 0, 0))],
        out_specs=col_spec,
        out_shape=jax.ShapeDtypeStruct((nb, length, D_MODEL), BF16),
        scratch_shapes=[pltpu.VMEM((2, length, LANES), BF16), pltpu.VMEM((2, length, LANES), BF16),
                        pltpu.VMEM((LANES, length), BF16),
                        pltpu.VMEM((2, length // ATT_TQ * ATT_TQ, ATT_TQ), F32)],
        compiler_params=_cparams(("parallel", "parallel")),
        name="fox_attn_prompt",
    )(q16, k16, v16, c3)


def _fox_new_bias_kernel(t_new, lf_ref, nt_ref):
    n = lf_ref.shape[0]
    r = lax.broadcasted_iota(jnp.int32, (n, n), 0)
    c = lax.broadcasted_iota(jnp.int32, (n, n), 1)
    same_seq = (r // t_new) == (c // t_new)
    bd = (same_seq & (c <= r)).astype(F32)
    cum = _select_dot(bd, lf_ref[...])
    er = lax.broadcasted_iota(jnp.int32, (FOX_HEADS, FOX_HEADS), 0)
    ec = lax.broadcasted_iota(jnp.int32, (FOX_HEADS, FOX_HEADS), 1)
    eye = (er == ec).astype(F32)
    nt_ref[...] = _select_dot(eye, cum, NT)


def _fox_new_bias(lf, t_new):
    n = lf.shape[0]
    return pl.pallas_call(
        functools.partial(_fox_new_bias_kernel, t_new),
        out_shape=jax.ShapeDtypeStruct((FOX_HEADS, n), F32),
        compiler_params=pltpu.CompilerParams(vmem_limit_bytes=VMEM_LIMIT),
        name="fox_new_bias",
    )(lf)


def _fox_sample_kernel(layer, t_new, n_pages, pt_ref, q_ref, kn_ref, vn_ref, nt_ref, ck_hbm, cv_hbm, clf_hbm,
                       o_ref, kbuf, vbuf, lfbuf, sems):
    b = pl.program_id(0)
    n_seq = pl.num_programs(0)
    rows = t_new * FOX_HEADS

    def page_copies(seq, slot):
        copies = []
        for p in range(n_pages):
            page = pt_ref[seq * n_pages + p]
            copies.append(pltpu.make_async_copy(ck_hbm.at[layer, page], kbuf.at[slot, :, pl.ds(p * PAGE, PAGE)],
                                                sems.at[slot, 0]))
            copies.append(pltpu.make_async_copy(cv_hbm.at[layer, page], vbuf.at[slot, :, pl.ds(p * PAGE, PAGE)],
                                                sems.at[slot, 1]))
            copies.append(pltpu.make_async_copy(clf_hbm.at[layer, page],
                                                lfbuf.at[slot, pl.ds(p * FOX_HEADS, FOX_HEADS)], sems.at[slot, 2]))
        return copies

    slot = b % 2

    @pl.when(b == 0)
    def _():
        for cp in page_copies(0, 0):
            cp.start()

    @pl.when(b + 1 < n_seq)
    def _():
        for cp in page_copies(b + 1, 1 - slot):
            cp.start()

    row = lax.broadcasted_iota(jnp.int32, (rows, D_MODEL), 0)
    lane = lax.broadcasted_iota(jnp.int32, (rows, D_MODEL), 1)
    own_head = (row % FOX_HEADS) == (lane // FOX_HEAD_DIM)
    q = q_ref[0]
    qrep = jnp.concatenate([jnp.broadcast_to(q[t:t + 1, :], (FOX_HEADS, D_MODEL)) for t in range(t_new)], axis=0)
    qexp = jnp.where(own_head, qrep, 0.0).astype(BF16)
    qf = qexp.astype(F32)
    kn = kn_ref[0].astype(BF16).astype(F32)
    vn = vn_ref[0]
    nt = nt_ref[0]
    row_t = lax.broadcasted_iota(jnp.int32, (rows, 1), 0) // FOX_HEADS
    cols = []
    for t2 in range(t_new):
        sc = jnp.sum(qf * kn[t2:t2 + 1, :], axis=-1, keepdims=True)
        sc = sc - jnp.concatenate([nt[:, t2:t2 + 1]] * t_new, axis=0)
        cols.append(jnp.where(row_t >= t2, sc, NEG))
    m_new = functools.reduce(jnp.maximum, cols)

    for cp in page_copies(b, slot):
        cp.wait()

    lft = lfbuf[slot]
    j = lax.broadcasted_iota(jnp.int32, (PAGE, PAGE), 0)
    key = lax.broadcasted_iota(jnp.int32, (PAGE, PAGE), 1)
    within = _select_dot((j > key).astype(F32), lft, sel_first=False)
    total = jnp.broadcast_to(within[:, 0:1] + lft[:, 0:1], (n_pages * FOX_HEADS, PAGE))
    pr = lax.broadcasted_iota(jnp.int32, (n_pages * FOX_HEADS, n_pages * FOX_HEADS), 0)
    pc = lax.broadcasted_iota(jnp.int32, (n_pages * FOX_HEADS, n_pages * FOX_HEADS), 1)
    later_page = ((pr % FOX_HEADS) == (pc % FOX_HEADS)) & (pc // FOX_HEADS > pr // FOX_HEADS)
    suf = within + _select_dot(later_page.astype(F32), total)
    bias = jnp.concatenate(
        [jnp.concatenate([suf[p * FOX_HEADS:(p + 1) * FOX_HEADS, :]] * t_new, axis=0) for p in range(n_pages)], axis=1)

    s = _dot(qf, kbuf[slot]) + bias
    m = jnp.maximum(m_new, jnp.max(s, axis=-1, keepdims=True))
    p_past = jnp.exp(s - m)
    p_new = [jnp.exp(sc - m) for sc in cols]
    l = jnp.sum(p_past, axis=-1, keepdims=True) + functools.reduce(jnp.add, p_new)
    acc = _dot(vbuf[slot], p_past.T).T
    acc = acc + functools.reduce(jnp.add, [p_new[t2] * vn[t2:t2 + 1, :] for t2 in range(t_new)])
    o = jnp.where(own_head, acc / l, 0.0)
    o_ref[0, 0:t_new, :] = o.reshape(t_new, FOX_HEADS, D_MODEL).sum(axis=1)
    o_ref[0, t_new:, :] = jnp.zeros((o_ref.shape[1] - t_new, D_MODEL), F32)


def _fox_sample(layer, t_new, page_table, q_new, k_new, v_new, nt, cache_kt, cache_vt, cache_lft):
    n_seq, tpad, _ = q_new.shape
    n_pages = page_table.shape[1]
    any_spec = pl.BlockSpec(memory_space=pl.ANY)
    grid_spec = pltpu.PrefetchScalarGridSpec(
        num_scalar_prefetch=1,
        grid=(n_seq,),
        in_specs=[pl.BlockSpec((1, tpad, D_MODEL), lambda b, pt: (b, 0, 0)),
                  pl.BlockSpec((1, tpad, D_MODEL), lambda b, pt: (b, 0, 0)),
                  pl.BlockSpec((1, tpad, D_MODEL), lambda b, pt: (b, 0, 0)),
                  pl.BlockSpec((1, FOX_HEADS, tpad), lambda b, pt: (b, 0, 0)),
                  any_spec, any_spec, any_spec],
        out_specs=pl.BlockSpec((1, tpad, D_MODEL), lambda b, pt: (b, 0, 0)),
        scratch_shapes=[pltpu.VMEM((2, D_MODEL, n_pages * PAGE), F32), pltpu.VMEM((2, D_MODEL, n_pages * PAGE), F32),
                        pltpu.VMEM((2, n_pages * FOX_HEADS, PAGE), F32), pltpu.SemaphoreType.DMA((2, 3))])
    return pl.pallas_call(
        functools.partial(_fox_sample_kernel, layer, t_new, n_pages),
        grid_spec=grid_spec,
        out_shape=jax.ShapeDtypeStruct((n_seq, tpad, D_MODEL), F32),
        compiler_params=_cparams(("arbitrary",)),
        name="fox_attn_sample",
    )(page_table.reshape(-1), q_new, k_new, v_new, nt, cache_kt, cache_vt, cache_lft)


def _valid_rows(tm, n_valid):
    row = pl.program_id(1) * tm + lax.broadcasted_iota(jnp.int32, (tm, 1), 0)
    return row < n_valid


def _fox_out_kernel(tm, n_valid, x_ref, o_ref, w_ref, g_ref, y_ref):
    m = _dot(o_ref[0].astype(BF16), w_ref[...])
    y_ref[0] = x_ref[0] + jnp.where(_valid_rows(tm, n_valid), _rms(m, g_ref[...]), 0.0)


def _fox_out(x, o, w, g, tm, n_valid):
    nb, length, _ = x.shape
    return pl.pallas_call(
        functools.partial(_fox_out_kernel, tm, n_valid),
        grid=(nb, length // tm),
        in_specs=[_row_spec(tm, D_MODEL), _row_spec(tm, D_MODEL),
                  _const_spec((D_MODEL, D_MODEL)), _const_spec((1, D_MODEL))],
        out_specs=_row_spec(tm, D_MODEL),
        out_shape=jax.ShapeDtypeStruct(x.shape, F32),
        compiler_params=_cparams(("parallel", "parallel")),
        name="fox_out",
    )(x, o, w, g)


def _gla_out_kernel(tm, n_valid, x_ref, o_ref, r_ref, gn_ref, w_ref, g_ref, y_ref):
    o = o_ref[0].astype(F32)
    r = r_ref[0].astype(F32)
    gn = gn_ref[...]
    parts = []
    for h in range(GLA_HEADS):
        cols = slice(h * GLA_DV_H, (h + 1) * GLA_DV_H)
        on = _rms(o[:, cols], gn[:, cols])
        rh = r[:, cols]
        parts.append((on * (rh * jax.nn.sigmoid(rh))).astype(BF16))
    m = _dot(jnp.concatenate(parts, axis=1), w_ref[...])
    y_ref[0] = x_ref[0] + jnp.where(_valid_rows(tm, n_valid), _rms(m, g_ref[...]), 0.0)


def _gla_out(x, o, r, gn, w, g, tm, n_valid):
    nb, length, _ = x.shape
    return pl.pallas_call(
        functools.partial(_gla_out_kernel, tm, n_valid),
        grid=(nb, length // tm),
        in_specs=[_row_spec(tm, D_MODEL), _row_spec(tm, GLA_DV), _row_spec(tm, GLA_DV),
                  _const_spec((1, GLA_DV)), _const_spec((GLA_DV, D_MODEL)), _const_spec((1, D_MODEL))],
        out_specs=_row_spec(tm, D_MODEL),
        out_shape=jax.ShapeDtypeStruct(x.shape, F32),
        compiler_params=_cparams(("parallel", "parallel")),
        name="gla_out",
    )(x, o, r, gn, w, g)


def _ffn_residual(y_ref, g2_ref, g3_ref, wup_ref, wdn_ref, xn_ref, acc_ref):
    xn_ref[...] = _rms(y_ref[0], g2_ref[...]).astype(BF16)
    acc_ref[...] = jnp.zeros(acc_ref.shape, F32)

    def chunk(c, _):
        xn = xn_ref[...]
        col = pl.multiple_of(c * FF_CHUNK, FF_CHUNK)
        h1 = _dot(xn, wup_ref[:, pl.ds(col, FF_CHUNK)])
        h2 = _dot(xn, wup_ref[:, pl.ds(D_FF + col, FF_CHUNK)])
        a = (h1 * jax.nn.sigmoid(h1) * h2).astype(BF16)
        acc_ref[...] += _dot(a, wdn_ref[c])
        return 0

    lax.fori_loop(0, N_FF_CHUNKS, chunk, 0)
    y_ref[0] = y_ref[0] + _rms(acc_ref[...], g3_ref[...])


def _ffn_kernel(x_ref, g2_ref, g3_ref, wup_ref, wdn_ref, y_ref, xn_ref, acc_ref):
    y_ref[0] = x_ref[0]
    _ffn_residual(y_ref, g2_ref, g3_ref, wup_ref, wdn_ref, xn_ref, acc_ref)


def _fox_out_ffn_kernel(tm, n_valid, x_ref, o_ref, wout_ref, g1_ref, g2_ref, g3_ref, wup_ref, wdn_ref,
                        y_ref, xn_ref, acc_ref):
    m = _dot(o_ref[0], wout_ref[...])
    y_ref[0] = x_ref[0] + jnp.where(_valid_rows(tm, n_valid), _rms(m, g1_ref[...]), 0.0)
    _ffn_residual(y_ref, g2_ref, g3_ref, wup_ref, wdn_ref, xn_ref, acc_ref)


def _ffn(x, g2, g3, wup, wdn, tm, mix=None):
    nb, length, _ = x.shape
    ffn_specs = [_const_spec((1, D_MODEL)), _const_spec((1, D_MODEL)),
                 _const_spec((D_MODEL, 2 * D_FF)), _const_spec((N_FF_CHUNKS, FF_CHUNK, D_MODEL))]
    if mix is None:
        body, in_specs, args = _ffn_kernel, [_row_spec(tm, D_MODEL)] + ffn_specs, (x, g2, g3, wup, wdn)
    else:
        o16, w_out, g1, n_valid = mix
        body = functools.partial(_fox_out_ffn_kernel, tm, n_valid)
        in_specs = [_row_spec(tm, D_MODEL), _row_spec(tm, D_MODEL), _const_spec((D_MODEL, D_MODEL)),
                    _const_spec((1, D_MODEL))] + ffn_specs
        args = (x, o16, w_out, g1, g2, g3, wup, wdn)
    return pl.pallas_call(
        body,
        grid=(nb, length // tm),
        in_specs=in_specs,
        out_specs=_row_spec(tm, D_MODEL),
        out_shape=jax.ShapeDtypeStruct(x.shape, F32),
        scratch_shapes=[pltpu.VMEM((tm, D_MODEL), BF16), pltpu.VMEM((tm, D_MODEL), F32)],
        compiler_params=_cparams(("parallel", "parallel")),
        name="ffn",
    )(*args)


def _gla_proj_kernel(x_ref, g_ref, wq_ref, wk_ref, wv_ref, wr_ref, wl_ref, wg2_ref, bg_ref,
                     q_ref, k_ref, v_ref, r_ref, gate_ref):
    xn = _rms(x_ref[0], g_ref[...]).astype(BF16)
    q_ref[0] = _dot(xn, wq_ref[...]) * GLA_SCALE
    k_ref[0] = _dot(xn, wk_ref[...])
    v_ref[0] = _dot(xn, wv_ref[...]).astype(BF16)
    r_ref[0] = _dot(xn, wr_ref[...]).astype(BF16)
    low = _dot(xn, wl_ref[...]).astype(BF16)
    z = _dot(low, wg2_ref[...]) + bg_ref[...]
    gate_ref[0] = _log_sigmoid(z) * (1.0 / GLA_TAU)


def _gla_proj(x, g, wq, wk, wv, wr, wl, wg2, bg, tm):
    nb, length, _ = x.shape
    sds = jax.ShapeDtypeStruct
    return pl.pallas_call(
        _gla_proj_kernel,
        grid=(nb, length // tm),
        in_specs=[_row_spec(tm, D_MODEL), _const_spec((1, D_MODEL)),
                  _const_spec((D_MODEL, GLA_DK)), _const_spec((D_MODEL, GLA_DK)),
                  _const_spec((D_MODEL, GLA_DV)), _const_spec((D_MODEL, GLA_DV)),
                  _const_spec((D_MODEL, LANES)), _const_spec((LANES, GLA_DK)), _const_spec((1, GLA_DK))],
        out_specs=[_row_spec(tm, GLA_DK), _row_spec(tm, GLA_DK), _row_spec(tm, GLA_DV),
                   _row_spec(tm, GLA_DV), _row_spec(tm, GLA_DK)],
        out_shape=[sds((nb, length, GLA_DK), F32), sds((nb, length, GLA_DK), F32),
                   sds((nb, length, GLA_DV), BF16), sds((nb, length, GLA_DV), BF16),
                   sds((nb, length, GLA_DK), F32)],
        compiler_params=_cparams(("parallel", "parallel")),
        name="gla_proj",
    )(x, g, wq, wk, wv, wr, wl, wg2, bg)


GLA_HPS = 2


def _gla_prompt_kernel(n_chunks, n_valid, q_ref, k_ref, g_ref, v_ref, o_ref, s_out_ref, s_ref):
    s_ref[...] = jnp.zeros(s_ref.shape, F32)
    r = lax.broadcasted_iota(jnp.int32, (GLA_CHUNK, GLA_CHUNK), 0)
    c = lax.broadcasted_iota(jnp.int32, (GLA_CHUNK, GLA_CHUNK), 1)
    causal = c <= r
    tri = causal.astype(F32)
    n_sub = GLA_CHUNK // GLA_SUB

    def chunk(ci, _):
        off = pl.multiple_of(ci * GLA_CHUNK, GLA_CHUNK)
        rows = pl.ds(off, GLA_CHUNK)
        pos = off + lax.broadcasted_iota(jnp.int32, (GLA_CHUNK, 1), 0)
        for hh in range(GLA_HPS):
            kcols = slice(hh * GLA_DK_H, (hh + 1) * GLA_DK_H)
            vcols = slice(hh * GLA_DV_H, (hh + 1) * GLA_DV_H)
            g = jnp.where(pos < n_valid, g_ref[0, rows, kcols], 0.0)
            q = q_ref[0, rows, kcols]
            k = k_ref[0, rows, kcols]
            v = v_ref[0, rows, vcols]
            big_g = _select_dot(tri, g)
            g_last = big_g[GLA_CHUNK - 1:GLA_CHUNK, :]
            g_last_col = big_g.T[:, GLA_CHUNK - 1:GLA_CHUNK]
            state = s_ref[hh]
            o = _dot((q * jnp.exp(big_g)).astype(BF16), state.astype(BF16))
            a_parts = []
            for i in range(n_sub):
                lo, hi = i * GLA_SUB, (i + 1) * GLA_SUB
                ref = big_g[lo - 1:lo, :] if i > 0 else jnp.zeros((1, GLA_DK_H), F32)
                q_i = (q[lo:hi, :] * jnp.exp(big_g[lo:hi, :] - ref)).astype(BF16)
                e = jnp.where(r < hi, jnp.exp(jnp.minimum(ref - big_g, EXP_CLAMP)), 0.0)
                a_parts.append(lax.dot_general(q_i, (k * e).astype(BF16), NT, preferred_element_type=F32))
            a = jnp.where(causal, jnp.concatenate(a_parts, axis=0), 0.0)
            o_ref[0, rows, vcols] = (o + _dot(a.astype(BF16), v)).astype(BF16)
            k_dec = (k * jnp.exp(g_last - big_g)).T.astype(BF16)
            s_ref[hh] = state * jnp.exp(g_last_col) + _dot(k_dec, v)
        return 0

    lax.fori_loop(0, n_chunks, chunk, 0)
    s_out_ref[0] = s_ref[...]


def _gla_prompt(q, k, g, v, n_valid):
    nb, length, _ = q.shape
    sds = jax.ShapeDtypeStruct
    kspec = pl.BlockSpec((1, length, GLA_HPS * GLA_DK_H), lambda b, h: (b, 0, h))
    vspec = pl.BlockSpec((1, length, GLA_HPS * GLA_DV_H), lambda b, h: (b, 0, h))
    return pl.pallas_call(
        functools.partial(_gla_prompt_kernel, length // GLA_CHUNK, n_valid),
        grid=(nb, GLA_HEADS // GLA_HPS),
        in_specs=[kspec, kspec, kspec, vspec],
        out_specs=[vspec, pl.BlockSpec((1, GLA_HPS, GLA_DK_H, GLA_DV_H), lambda b, h: (b, h, 0, 0))],
        out_shape=[sds((nb, length, GLA_DV), BF16), sds((nb, GLA_HEADS, GLA_DK_H, GLA_DV_H), F32)],
        scratch_shapes=[pltpu.VMEM((GLA_HPS, GLA_DK_H, GLA_DV_H), F32)],
        compiler_params=_cparams(("parallel", "parallel")),
        name="gla_prompt",
    )(q, k, g, v)


def _gla_sample_kernel(sb, t_new, q_ref, k_ref, g_ref, v_ref, s_in_ref, o_ref, s_out_ref):
    tpad = q_ref.shape[1]
    rowi = lax.broadcasted_iota(jnp.int32, (tpad, GLA_DK_H), 0)
    rowc = lax.broadcasted_iota(jnp.int32, (tpad, 1), 0)
    for i in range(sb):
        for h in range(GLA_HEADS):
            kcols = slice(h * GLA_DK_H, (h + 1) * GLA_DK_H)
            vcols = slice(h * GLA_DV_H, (h + 1) * GLA_DV_H)
            g = g_ref[i, :, kcols]
            q = q_ref[i, :, kcols]
            k = k_ref[i, :, kcols]
            v = v_ref[i, :, vcols]
            big_g = jnp.zeros((tpad, GLA_DK_H), F32)
            for u in range(t_new):
                big_g = big_g + jnp.where(rowi >= u, g[u:u + 1, :], 0.0)
            state = s_in_ref[0, i, h]
            o = _dot(q * jnp.exp(big_g), state)
            for s in range(t_new):
                w = jnp.exp(jnp.where(rowi >= s, big_g - big_g[s:s + 1, :], 0.0))
                a = jnp.sum(q * k[s:s + 1, :] * w, axis=-1, keepdims=True)
                o = o + jnp.where(rowc >= s, a, 0.0) * v[s:s + 1, :]
            o_ref[i, :, vcols] = o
            g_last = big_g[t_new - 1:t_new, :]
            k_dec = k * jnp.exp(g_last - big_g)
            g_last_col = jnp.broadcast_to(g_last, (tpad, GLA_DK_H)).T[:, 0:1]
            s_out_ref[0, i, h] = state * jnp.exp(g_last_col) + lax.dot_general(k_dec, v, TN, preferred_element_type=F32)


def _gla_sample(layer, t_new, q, k, g, v, state_all, sb):
    n_seq, tpad, _ = q.shape
    sds = jax.ShapeDtypeStruct
    kspec = pl.BlockSpec((sb, tpad, GLA_DK), lambda b: (b, 0, 0))
    vspec = pl.BlockSpec((sb, tpad, GLA_DV), lambda b: (b, 0, 0))
    sspec_in = pl.BlockSpec((1, sb, GLA_HEADS, GLA_DK_H, GLA_DV_H), lambda b: (layer, b, 0, 0, 0))
    sspec_out = pl.BlockSpec((1, sb, GLA_HEADS, GLA_DK_H, GLA_DV_H), lambda b: (0, b, 0, 0, 0))
    o, s_new = pl.pallas_call(
        functools.partial(_gla_sample_kernel, sb, t_new),
        grid=(n_seq // sb,),
        in_specs=[kspec, kspec, kspec, vspec, sspec_in],
        out_specs=[vspec, sspec_out],
        out_shape=[sds((n_seq, tpad, GLA_DV), F32), sds((1, n_seq, GLA_HEADS, GLA_DK_H, GLA_DV_H), F32)],
        compiler_params=_cparams(("parallel",)),
        name="gla_sample",
    )(q, k, g, v, state_all)
    return o, s_new[0]


def kernel(x_prompt, x_sample, cache_k, cache_v, cache_logf, state_gla, page_table, meta, norm_g, fox_w_in, fox_b_f, fox_w_out, gla_w_in, gla_w_gate2, gla_b_gate, gla_norm_g, gla_w_out, ffn_w_up, ffn_w_down):
    n_b, seq, _ = x_prompt.shape
    n_seq, t_new, _ = x_sample.shape
    depth = norm_g.shape[0]
    n_valid = N_META_TOK + seq
    length = -(-n_valid // LANES) * LANES
    tm_p = length // 4
    tm_big = length // 2
    n_s_rows = n_seq * t_new
    dt = x_prompt.dtype

    xp = jnp.concatenate([jnp.broadcast_to(meta.astype(dt)[None], (n_b, N_META_TOK, D_MODEL)), x_prompt,
                          jnp.zeros((n_b, length - n_valid, D_MODEL), dt)], axis=1)
    xs = x_sample.reshape(1, n_s_rows, D_MODEL)
    tpad = 8

    def pad_rows(a):
        return jnp.pad(a.reshape(n_seq, t_new, a.shape[-1]), ((0, 0), (0, tpad - t_new), (0, 0)))

    n_pool = cache_k.shape[1]
    cache_kt = jnp.transpose(cache_k, (0, 1, 3, 4, 2)).reshape(cache_k.shape[0], n_pool, D_MODEL, PAGE)
    cache_vt = jnp.transpose(cache_v, (0, 1, 3, 4, 2)).reshape(cache_v.shape[0], n_pool, D_MODEL, PAGE)
    cache_lft = jnp.swapaxes(cache_logf, 2, 3)

    n_fox = (depth + 1) // 2
    kv_p = [jnp.zeros((n_fox, n_b, n_valid, D_MODEL), F32) for _ in range(2)]
    kv_s = [jnp.zeros((n_fox, 1, n_s_rows, D_MODEL), F32) for _ in range(2)]
    nf_p, ns_p, nf_s, ns_s = [], [], [], []
    for i in range(depth):
        g = norm_g[i].reshape(4, 1, D_MODEL)
        j = i // 2
        if i % 2 == 0:
            w_in = fox_w_in[j]
            wq = w_in[:, :D_MODEL].astype(BF16)
            wk = w_in[:, D_MODEL:2 * D_MODEL].astype(BF16)
            wv = w_in[:, 2 * D_MODEL:3 * D_MODEL].astype(BF16)
            wf = jnp.pad(w_in[:, 3 * D_MODEL:], ((0, 0), (0, LANES - FOX_HEADS))).astype(BF16)
            bf = fox_b_f[j].reshape(1, FOX_HEADS)
            w_out = fox_w_out[j].astype(BF16)
            q16, *kv_p, k16, v16, lf = _fox_proj(xp, g[0], wq, wk, wv, wf, bf, tm_big, FOX_SCALE * LOG2E, j, *kv_p)
            o16 = _fox_attn(q16, k16, v16, _fox_cumsum(lf))
            mix = (o16, w_out, g[1], n_valid)
            nf_p.append(lf[:, :n_valid])
            q16s, *kv_s, _, _, lfs = _fox_proj(xs, g[0], wq, wk, wv, wf, bf, n_s_rows, FOX_SCALE, j, *kv_s)
            k32s, v32s = kv_s[0][j], kv_s[1][j]
            nt = _fox_new_bias(lfs[0], t_new)
            nt = jnp.transpose(nt.reshape(FOX_HEADS, n_seq, t_new), (1, 0, 2))
            nt = jnp.pad(nt, ((0, 0), (0, 0), (0, tpad - t_new)))
            os_ = _fox_sample(j, t_new, page_table, pad_rows(q16s.astype(F32)), pad_rows(k32s), pad_rows(v32s), nt,
                              cache_kt, cache_vt, cache_lft)
            xs = _fox_out(xs, os_[:, :t_new].reshape(1, n_s_rows, D_MODEL), w_out, g[1], n_s_rows, n_s_rows)
            nf_s.append(lfs.reshape(n_seq, t_new, FOX_HEADS))
        else:
            w_in = gla_w_in[j]
            o1, o2, o3, o4 = GLA_DK, 2 * GLA_DK, 2 * GLA_DK + GLA_DV, 2 * GLA_DK + 2 * GLA_DV
            wq = w_in[:, :o1].astype(BF16)
            wk = w_in[:, o1:o2].astype(BF16)
            wv = w_in[:, o2:o3].astype(BF16)
            wr = w_in[:, o3:o4].astype(BF16)
            wl = jnp.pad(w_in[:, o4:], ((0, 0), (0, LANES - GLA_RANK))).astype(BF16)
            wg2 = jnp.pad(gla_w_gate2[j], ((0, LANES - GLA_RANK), (0, 0))).astype(BF16)
            bg = gla_b_gate[j].reshape(1, GLA_DK)
            gn = gla_norm_g[j].reshape(1, GLA_DV)
            w_out = gla_w_out[j].astype(BF16)
            q, k, v, r, gate = _gla_proj(xp, g[0], wq, wk, wv, wr, wl, wg2, bg, tm_big)
            mix = None
            o, s_fin = _gla_prompt(q, k, gate, v, n_valid)
            xp = _gla_out(xp, o, r, gn, w_out, g[1], tm_p, n_valid)
            ns_p.append(s_fin)
            q, k, v, r, gate = _gla_proj(xs, g[0], wq, wk, wv, wr, wl, wg2, bg, n_s_rows)
            o, s_new = _gla_sample(j, t_new, pad_rows(q), pad_rows(k), pad_rows(gate), pad_rows(v.astype(F32)),
                                   state_gla, 4)
            xs = _gla_out(xs, o[:, :t_new].reshape(1, n_s_rows, GLA_DV), r, gn, w_out, g[1], n_s_rows, n_s_rows)
            ns_s.append(s_new)
        wup = ffn_w_up[i].astype(BF16)
        wdn = ffn_w_down[i].astype(BF16).reshape(N_FF_CHUNKS, FF_CHUNK, D_MODEL)
        xp = _ffn(xp, g[2], g[3], wup, wdn, tm_big, mix)
        xs = _ffn(xs, g[2], g[3], wup, wdn, n_s_rows)

    y_prompt = xp[:, N_META_TOK:n_valid]
    y_sample = xs.reshape(n_seq, t_new, D_MODEL)
    heads = lambda a, rows: a.reshape(n_fox, -1, rows, FOX_HEADS, FOX_HEAD_DIM)
    return (y_prompt, y_sample,
            heads(kv_p[0], n_valid), heads(kv_p[1], n_valid), jnp.stack(nf_p), jnp.stack(ns_p),
            heads(kv_s[0], t_new), heads(kv_s[1], t_new), jnp.stack(nf_s), jnp.stack(ns_s))
```
